```python
import math
import jax, jax.numpy as jnp
from jax import lax
import numpy as np

D_MODEL = 1024
BATCH = 4
SEQ = 8192
DEPTH = 4

CHUNK = 64
QBLOCK = 128
HEAD_DIM = 64
D_MIX = D_MODEL
SSM_WIDTH = D_MIX // 4
RWKV_WIDTH = D_MIX // 4
GDN_WIDTH = D_MIX // 4
SB_WIDTH = D_MIX - SSM_WIDTH - RWKV_WIDTH - GDN_WIDTH
SSM_HEADS = SSM_WIDTH // HEAD_DIM
SSM_GROUPS = 2
SSM_STATE = 128
SSM_CONV_DIM = SSM_WIDTH + 2 * SSM_GROUPS * SSM_STATE
CONV_K = 4
RWKV_HEADS = RWKV_WIDTH // HEAD_DIM
RWKV_DECAY_LORA = 32
RWKV_AAA_LORA = 32
RWKV_GATE_LORA = 64
RWKV_IN = 3 * RWKV_WIDTH + RWKV_DECAY_LORA + RWKV_AAA_LORA + RWKV_GATE_LORA
RWKV_GN_EPS = 64e-5
RWKV_DECAY_OFFSET = 0.5
GDN_HEADS = GDN_WIDTH // HEAD_DIM
SB_HEADS = SB_WIDTH // HEAD_DIM
SSM_IN = SSM_WIDTH + SSM_CONV_DIM + SSM_HEADS
GDN_IN = 4 * GDN_WIDTH + 2 * GDN_HEADS
SB_IN = 3 * SB_WIDTH
N_IN = SSM_IN + RWKV_IN + GDN_IN + SB_IN
D_FF = ((8 * D_MODEL // 3 + 127) // 128) * 128
PLE_DIM = 256
NORM_EPS = 1e-6
L2_EPS = 1e-6

kernel_name = "hybrid_ssd_rwkv7_gdn_stickbreak_macaron"


def split_sizes(x, sizes):
    idx, acc = [], 0
    for sz in sizes[:-1]:
        acc += sz
        idx.append(acc)
    return jnp.split(x, idx, axis=-1)


def rms_norm(x, gain):
    xf = x.astype(jnp.float32)
    y = xf * lax.rsqrt(jnp.mean(xf * xf, axis=-1, keepdims=True) + NORM_EPS)
    return (y * gain.astype(jnp.float32)).astype(x.dtype)


def l2_normalize(x):
    return x * lax.rsqrt(jnp.sum(x * x, axis=-1, keepdims=True) + L2_EPS)


def swiglu(u, w_gate, w_up, w_down):
    return (jax.nn.silu(u @ w_gate) * (u @ w_up)) @ w_down


def causal_depthwise_conv(x, w):
    k = w.shape[0]
    return lax.conv_general_dilated(
        x, w[:, None, :].astype(x.dtype), window_strides=(1,), padding=[(k - 1, 0)],
        dimension_numbers=("NWC", "WIO", "NWC"), feature_group_count=x.shape[-1])


def ssd_chunked(x, dt, a, bm, cm):
    b, s, h, pd = x.shape
    n = bm.shape[-1]
    nc = s // CHUNK
    xc = (x * dt[..., None]).reshape(b, nc, CHUNK, h, pd)
    bc = bm.reshape(b, nc, CHUNK, h, n)
    cc = cm.reshape(b, nc, CHUNK, h, n)
    la = jnp.moveaxis((dt * a).reshape(b, nc, CHUNK, h), 3, 2)
    la_cum = jnp.cumsum(la, axis=-1)
    incl = jnp.tril(jnp.ones((CHUNK, CHUNK), dtype=bool))
    seg = jnp.exp(jnp.where(incl, la_cum[..., :, None] - la_cum[..., None, :], -jnp.inf))
    scores = jnp.einsum("bclhn,bcshn->bchls", cc, bc) * seg
    y_diag = jnp.einsum("bchls,bcshp->bclhp", scores, xc)
    to_end = jnp.exp(la_cum[..., -1:] - la_cum)
    chunk_states = jnp.einsum("bclhn,bchl,bclhp->bchpn", bc, to_end, xc)
    chunk_decay = jnp.exp(la_cum[..., -1])

    def step(state, inp):
        st, dec = inp
        return state * dec[..., None, None] + st, state

    _, prev = lax.scan(step, jnp.zeros((b, h, pd, n), x.dtype),
                       (jnp.moveaxis(chunk_states, 1, 0), jnp.moveaxis(chunk_decay, 1, 0)))
    prev = jnp.moveaxis(prev, 0, 1)
    y_off = jnp.einsum("bclhn,bchpn,bchl->bclhp", cc, prev, jnp.exp(la_cum))
    return (y_diag + y_off).reshape(b, s, h, pd)


def rwkv7_scan(r, w, k, v, a, bv):
    bsz, s, h, d = r.shape

    def step(state, inp):
        r_t, w_t, k_t, v_t, a_t, b_t = inp
        sa = jnp.einsum("bhvk,bhk->bhv", state, a_t)
        state = (state * w_t[:, :, None, :] + sa[..., None] * b_t[:, :, None, :]
                 + v_t[..., None] * k_t[:, :, None, :])
        return state, jnp.einsum("bhvk,bhk->bhv", state, r_t)

    xs = tuple(jnp.moveaxis(t, 1, 0) for t in (r, w, k, v, a, bv))
    _, y = lax.scan(step, jnp.zeros((bsz, h, d, d), r.dtype), xs)
    return jnp.moveaxis(y, 0, 1)


def gated_delta_rule_chunked(q, k, v, g, beta):
    b, s, h, dk = q.shape
    dv = v.shape[-1]
    nc = s // CHUNK

    def chunks(t):
        return jnp.moveaxis(t.reshape((b, nc, CHUNK, h) + t.shape[3:]), 3, 1)

    qc, kc, vc = chunks(q), chunks(k), chunks(v)
    gc = jnp.cumsum(chunks(g), axis=-1)
    bc = chunks(beta)
    incl = jnp.tril(jnp.ones((CHUNK, CHUNK), dtype=bool))
    strict = jnp.tril(jnp.ones((CHUNK, CHUNK), dtype=bool), -1)
    decay = jnp.exp(jnp.where(incl, gc[..., :, None] - gc[..., None, :], -jnp.inf))
    k_beta = kc * bc[..., None]
    a_mat = jnp.where(strict, jnp.einsum("bhcld,bhcsd->bhcls", k_beta, kc) * decay, 0.0)
    eye = jnp.eye(CHUNK, dtype=q.dtype)
    rhs = jnp.concatenate([vc * bc[..., None], k_beta * jnp.exp(gc)[..., None]], axis=-1)
    sol = lax.linalg.triangular_solve(a_mat + eye, rhs, left_side=True, lower=True,
                                      unit_diagonal=True)
    u, w = sol[..., :dv], sol[..., dv:]
    qk = jnp.einsum("bhcld,bhcsd->bhcls", qc, kc) * decay
    q_dec = qc * jnp.exp(gc)[..., None]
    k_dec = kc * jnp.exp(gc[..., -1:] - gc)[..., None]
    g_last = jnp.exp(gc[..., -1])

    def step(state, inp):
        u_c, w_c, qk_c, q_c, k_c, gl = inp
        v_new = u_c - jnp.einsum("bhlk,bhkv->bhlv", w_c, state)
        o = jnp.einsum("bhlk,bhkv->bhlv", q_c, state) + jnp.einsum("bhls,bhsv->bhlv", qk_c, v_new)
        state = state * gl[..., None, None] + jnp.einsum("bhlk,bhlv->bhkv", k_c, v_new)
        return state, o

    xs = tuple(jnp.moveaxis(t, 2, 0) for t in (u, w, qk, q_dec, k_dec, g_last))
    _, o = lax.scan(step, jnp.zeros((b, h, dk, dv), q.dtype), xs)
    return jnp.transpose(o, (1, 0, 3, 2, 4)).reshape(b, s, h, dv)


def stick_breaking_attention(q, k, v):
    b, s, h, d = q.shape
    nb = s // QBLOCK
    scale = d ** -0.5
    qb = jnp.transpose(q.reshape(b, nb, QBLOCK, h, d), (1, 0, 3, 2, 4))
    kt = jnp.transpose(k, (0, 2, 1, 3))
    vt = jnp.transpose(v, (0, 2, 1, 3))
    key_pos = jnp.arange(s)

    def block(args):
        q_blk, start = args
        z = jnp.einsum("bhqd,bhkd->bhqk", q_blk, kt) * scale
        q_pos = start + jnp.arange(QBLOCK)
        mask = key_pos[None, :] < q_pos[:, None]
        log_1m = jnp.where(mask, jax.nn.log_sigmoid(-z), 0.0)
        suffix = lax.cumsum(log_1m, axis=3, reverse=True) - log_1m
        weight = jnp.where(mask, jnp.exp(jax.nn.log_sigmoid(z) + suffix), 0.0)
        return jnp.einsum("bhqk,bhkd->bhqd", weight, vt)

    out = lax.map(block, (qb, jnp.arange(nb) * QBLOCK))
    return jnp.transpose(out, (1, 0, 3, 2, 4)).reshape(b, s, h, d)


def ssd_group(pa, conv_w, conv_b, dt_bias, a_log, d_skip, norm_w):
    b, s, _ = pa.shape
    z, xbc, dt = split_sizes(pa, [SSM_WIDTH, SSM_CONV_DIM, SSM_HEADS])
    xbc = jax.nn.silu(causal_depthwise_conv(xbc, conv_w) + conv_b)
    xs, bm, cm = split_sizes(xbc, [SSM_WIDTH, SSM_GROUPS * SSM_STATE, SSM_GROUPS * SSM_STATE])
    xs = xs.reshape(b, s, SSM_HEADS, HEAD_DIM)
    rep = SSM_HEADS // SSM_GROUPS
    bm = jnp.repeat(bm.reshape(b, s, SSM_GROUPS, SSM_STATE), rep, axis=2)
    cm = jnp.repeat(cm.reshape(b, s, SSM_GROUPS, SSM_STATE), rep, axis=2)
    dt = jax.nn.softplus(dt + dt_bias)
    y = ssd_chunked(xs, dt, -jnp.exp(a_log), bm, cm) + d_skip[:, None] * xs
    y = (y.reshape(b, s, SSM_WIDTH) * jax.nn.silu(z)).reshape(b, s, SSM_GROUPS, -1)
    y = y * lax.rsqrt(jnp.mean(y * y, axis=-1, keepdims=True) + NORM_EPS)
    return y.reshape(b, s, SSM_WIDTH) * norm_w


def rwkv7_group(pb, mu, w0, w_up, a0, a_up, g_up, k_k, k_a, r_k, ln_w, ln_b):
    b, s, _ = pb.shape
    shifted = jnp.pad(pb, ((0, 0), (1, 0), (0, 0)))[:, :-1]
    pb = pb + mu * (shifted - pb)
    r, k, v, xw, xa, xg = split_sizes(
        pb, [RWKV_WIDTH] * 3 + [RWKV_DECAY_LORA, RWKV_AAA_LORA, RWKV_GATE_LORA])
    w_log = -jax.nn.softplus(-(w0 + jnp.tanh(xw) @ w_up)) - RWKV_DECAY_OFFSET
    decay = jnp.exp(-jnp.exp(w_log))
    a = jax.nn.sigmoid(a0 + xa @ a_up)
    g = jax.nn.sigmoid(xg) @ g_up
    heads = lambda t: t.reshape(b, s, RWKV_HEADS, HEAD_DIM)
    kk = l2_normalize(heads(k * k_k))
    k = k * (1.0 + (a - 1.0) * k_a)
    rh, kh, vh, ah = heads(r), heads(k), heads(v), heads(a)
    y = rwkv7_scan(rh, heads(decay), kh, vh, -kk, kk * ah)
    mean = jnp.mean(y, axis=-1, keepdims=True)
    var = jnp.mean(jnp.square(y - mean), axis=-1, keepdims=True)
    y = ((y - mean) * lax.rsqrt(var + RWKV_GN_EPS)).reshape(b, s, RWKV_WIDTH) * ln_w + ln_b
    bonus = jnp.sum(rh * kh * r_k, axis=-1, keepdims=True) * vh
    return (y + bonus.reshape(b, s, RWKV_WIDTH)) * g


def gdn_group(pc, conv_w, a_log, dt_bias, norm_w):
    b, s, _ = pc.shape
    qkv, z, beta_in, a_in = split_sizes(pc, [3 * GDN_WIDTH, GDN_WIDTH, GDN_HEADS, GDN_HEADS])
    qkv = jax.nn.silu(causal_depthwise_conv(qkv, conv_w))
    q, k, v = (t.reshape(b, s, GDN_HEADS, HEAD_DIM) for t in jnp.split(qkv, 3, axis=-1))
    q = l2_normalize(q) * (HEAD_DIM ** -0.5)
    k = l2_normalize(k)
    beta = jax.nn.sigmoid(beta_in)
    g = -jnp.exp(a_log) * jax.nn.softplus(a_in + dt_bias)
    o = gated_delta_rule_chunked(q, k, v, g, beta)
    o = o * lax.rsqrt(jnp.mean(o * o, axis=-1, keepdims=True) + NORM_EPS) * norm_w
    return (o * jax.nn.silu(z.reshape(b, s, GDN_HEADS, HEAD_DIM))).reshape(b, s, GDN_WIDTH)


def stick_breaking_group(pd):
    b, s, _ = pd.shape
    q, k, v = (t.reshape(b, s, SB_HEADS, HEAD_DIM) for t in jnp.split(pd, 3, axis=-1))
    return stick_breaking_attention(q, k, v).reshape(b, s, SB_WIDTH)


def hybrid_mix(u, w_in, ssm_conv_w, ssm_conv_b, ssm_dt_bias, ssm_a_log, ssm_d, ssm_norm,
               rwkv_mu, rwkv_w0, rwkv_w_up, rwkv_a0, rwkv_a_up, rwkv_g_up, rwkv_k_k, rwkv_k_a,
               rwkv_r_k, rwkv_ln_w, rwkv_ln_b, gdn_conv_w, gdn_a_log, gdn_dt_bias, gdn_norm, w_out):
    f32 = lambda t: t.astype(jnp.float32)
    proj = f32(u @ w_in)
    pa, pb, pc, pd = split_sizes(proj, [SSM_IN, RWKV_IN, GDN_IN, SB_IN])
    ya = ssd_group(pa, f32(ssm_conv_w), f32(ssm_conv_b), f32(ssm_dt_bias), f32(ssm_a_log),
                   f32(ssm_d), f32(ssm_norm))
    yb = rwkv7_group(pb, f32(rwkv_mu), f32(rwkv_w0), f32(rwkv_w_up), f32(rwkv_a0), f32(rwkv_a_up),
                     f32(rwkv_g_up), f32(rwkv_k_k), f32(rwkv_k_a), f32(rwkv_r_k), f32(rwkv_ln_w),
                     f32(rwkv_ln_b))
    yc = gdn_group(pc, f32(gdn_conv_w), f32(gdn_a_log), f32(gdn_dt_bias), f32(gdn_norm))
    yd = stick_breaking_group(pd)
    y = jnp.concatenate([ya, yb, yc, yd], axis=-1).astype(u.dtype)
    return y @ w_out


def setup_inputs(seed: int = 0) -> dict:
    key = jax.random.key(seed)
    it = iter(jax.random.split(key, 48))
    f = jnp.float32

    def nrm(shape, scale):
        return scale * jax.random.normal(next(it), shape, f)

    def gain(shape):
        return 1.0 + 0.02 * jax.random.normal(next(it), shape, f)

    def unif(shape, lo, hi):
        return jax.random.uniform(next(it), shape, f, lo, hi)

    def dt_bias(shape):
        dt = jnp.exp(unif(shape, math.log(0.001), math.log(0.1)))
        return dt + jnp.log(-jnp.expm1(-dt))

    def a_log(shape):
        return jnp.log(unif(shape, 1.0, 16.0))

    L = DEPTH
    return {
        "x": nrm((BATCH, SEQ, D_MODEL), 1.0),
        "p": nrm((DEPTH, BATCH, SEQ, PLE_DIM), 1.0),
        "ffn1_norm": gain((L, D_MODEL)),
        "ffn1_w_gate": nrm((L, D_MODEL, D_FF), D_MODEL ** -0.5),
        "ffn1_w_up": nrm((L, D_MODEL, D_FF), D_MODEL ** -0.5),
        "ffn1_w_down": nrm((L, D_FF, D_MODEL), D_FF ** -0.5),
        "mix_norm": gain((L, D_MODEL)),
        "w_in": nrm((L, D_MODEL, N_IN), D_MODEL ** -0.5),
        "ssm_conv_w": nrm((L, CONV_K, SSM_CONV_DIM), CONV_K ** -0.5),
        "ssm_conv_b": nrm((L, SSM_CONV_DIM), 0.01),
        "ssm_dt_bias": dt_bias((L, SSM_HEADS)),
        "ssm_a_log": a_log((L, SSM_HEADS)),
        "ssm_d": gain((L, SSM_HEADS)),
        "ssm_norm": gain((L, SSM_WIDTH)),
        "rwkv_mu": unif((L, RWKV_IN), 0.0, 1.0),
        "rwkv_w0": unif((L, RWKV_WIDTH), -6.5, -1.5),
        "rwkv_w_up": nrm((L, RWKV_DECAY_LORA, RWKV_WIDTH), 0.1),
        "rwkv_a0": nrm((L, RWKV_WIDTH), 0.1),
        "rwkv_a_up": nrm((L, RWKV_AAA_LORA, RWKV_WIDTH), RWKV_AAA_LORA ** -0.5),
        "rwkv_g_up": nrm((L, RWKV_GATE_LORA, RWKV_WIDTH), RWKV_GATE_LORA ** -0.5),
        "rwkv_k_k": 0.85 + nrm((L, RWKV_WIDTH), 0.02),
        "rwkv_k_a": gain((L, RWKV_WIDTH)),
        "rwkv_r_k": nrm((L, RWKV_HEADS, HEAD_DIM), 0.1),
        "rwkv_ln_w": gain((L, RWKV_WIDTH)),
        "rwkv_ln_b": nrm((L, RWKV_WIDTH), 0.01),
        "gdn_conv_w": nrm((L, CONV_K, 3 * GDN_WIDTH), CONV_K ** -0.5),
        "gdn_a_log": a_log((L, GDN_HEADS)),
        "gdn_dt_bias": dt_bias((L, GDN_HEADS)),
        "gdn_norm": gain((L, HEAD_DIM)),
        "w_out": nrm((L, D_MIX, D_MODEL), D_MIX ** -0.5),
        "ffn2_norm": gain((L, D_MODEL)),
        "ffn2_w_gate": nrm((L, D_MODEL, D_FF), D_MODEL ** -0.5),
        "ffn2_w_up": nrm((L, D_MODEL, D_FF), D_MODEL ** -0.5),
        "ffn2_w_down": nrm((L, D_FF, D_MODEL), D_FF ** -0.5),
        "ple_norm": gain((L, D_MODEL)),
        "ple_w_gate": nrm((L, D_MODEL, D_MODEL), D_MODEL ** -0.5),
        "ple_w_proj": nrm((L, PLE_DIM, D_MODEL), PLE_DIM ** -0.5),
        "final_norm": gain((D_MODEL,)),
    }


def reference(x, p, ffn1_norm, ffn1_w_gate, ffn1_w_up, ffn1_w_down, mix_norm, w_in,
              ssm_conv_w, ssm_conv_b, ssm_dt_bias, ssm_a_log, ssm_d, ssm_norm,
              rwkv_mu, rwkv_w0, rwkv_w_up, rwkv_a0, rwkv_a_up, rwkv_g_up, rwkv_k_k, rwkv_k_a,
              rwkv_r_k, rwkv_ln_w, rwkv_ln_b, gdn_conv_w, gdn_a_log, gdn_dt_bias, gdn_norm,
              w_out, ffn2_norm, ffn2_w_gate, ffn2_w_up, ffn2_w_down,
              ple_norm, ple_w_gate, ple_w_proj, final_norm):
    h = x
    for i in range(DEPTH):
        h = h + 0.5 * swiglu(rms_norm(h, ffn1_norm[i]), ffn1_w_gate[i], ffn1_w_up[i], ffn1_w_down[i])
        h = h + hybrid_mix(rms_norm(h, mix_norm[i]), w_in[i],
                           ssm_conv_w[i], ssm_conv_b[i], ssm_dt_bias[i], ssm_a_log[i], ssm_d[i], ssm_norm[i],
                           rwkv_mu[i], rwkv_w0[i], rwkv_w_up[i], rwkv_a0[i], rwkv_a_up[i], rwkv_g_up[i],
                           rwkv_k_k[i], rwkv_k_a[i], rwkv_r_k[i], rwkv_ln_w[i], rwkv_ln_b[i],
                           gdn_conv_w[i], gdn_a_log[i], gdn_dt_bias[i], gdn_norm[i], w_out[i])
        h = h + 0.5 * swiglu(rms_norm(h, ffn2_norm[i]), ffn2_w_gate[i], ffn2_w_up[i], ffn2_w_down[i])
        gate = jax.nn.sigmoid(rms_norm(h, ple_norm[i]) @ ple_w_gate[i])
        h = h + (p[i] @ ple_w_proj[i]) * gate
    return rms_norm(h, final_norm)
```

```python
import functools

import jax
import jax.numpy as jnp
from jax import lax
from jax.experimental import pallas as pl
from jax.experimental.pallas import tpu as pltpu

F32 = jnp.float32
BF16 = jnp.bfloat16

HEAD_DIM = 64
N_HEADS = 4
GROUP_W = HEAD_DIM * N_HEADS
CHUNK = 64
CONV_K = 4
SSM_STATE = 128
NORM_EPS = 1e-6
L2_EPS = 1e-6
RWKV_GN_EPS = 64e-5
RWKV_DECAY_OFFSET = 0.5
HALO = 8
LANE = 128
VMEM_LIMIT = 56 * 1024 * 1024


def _dg(a, b, ca, cb):
    return lax.dot_general(a, b, (((ca,), (cb,)), ((), ())), preferred_element_type=F32)


_DIMS = {"nn": (1, 0), "nt": (1, 1), "tn": (0, 0)}


def _mm(a, b, kind="nn"):
    ca, cb = _DIMS[kind]
    return _dg(a.astype(BF16), b.astype(BF16), ca, cb)


def _split2(x):
    hi = x.astype(BF16)
    lo = (x - hi.astype(F32)).astype(BF16)
    return hi, lo


def _split3(x):
    hi = x.astype(BF16)
    r = x - hi.astype(F32)
    mid = r.astype(BF16)
    lo = (r - mid.astype(F32)).astype(BF16)
    return hi, mid, lo


def _mm3(a, b, kind="nn"):
    ca, cb = _DIMS[kind]
    ah, al = _split2(a)
    bh, bl = _split2(b)
    return _dg(ah, bh, ca, cb) + (_dg(ah, bl, ca, cb) + _dg(al, bh, ca, cb))


def _mm_sel_l(sel, b, kind="nn"):
    ca, cb = _DIMS[kind]
    s = sel.astype(BF16)
    h, m, l = _split3(b)
    return _dg(s, h, ca, cb) + (_dg(s, m, ca, cb) + _dg(s, l, ca, cb))


def _mm_sel_r(a, sel, kind="nn"):
    ca, cb = _DIMS[kind]
    s = sel.astype(BF16)
    h, m, l = _split3(a)
    return _dg(h, s, ca, cb) + (_dg(m, s, ca, cb) + _dg(l, s, ca, cb))


def _sigmoid(x):
    return 1.0 / (1.0 + jnp.exp(-x))


def _silu(x):
    return x * _sigmoid(x)


def _softplus(x):
    return jnp.maximum(x, 0.0) + jnp.log1p(jnp.exp(-jnp.abs(x)))


def _iota(shape, dim):
    return lax.broadcasted_iota(jnp.int32, shape, dim)


def _tile_rows(x, n):
    return jnp.concatenate([x] * n, axis=0)


def _rms(x, gain_row):
    ms = jnp.mean(x * x, axis=-1, keepdims=True)
    return x * lax.rsqrt(ms + NORM_EPS) * gain_row


def _chunk_masks():
    r = _iota((GROUP_W, GROUP_W), 0)
    c = _iota((GROUP_W, GROUP_W), 1)
    same_head = (r // HEAD_DIM) == (c // HEAD_DIM)
    lc = _iota((CHUNK, GROUP_W), 0)
    sc = _iota((CHUNK, GROUP_W), 1) % CHUNK
    incl = lc >= sc
    strict = lc > sc
    ltri = (_iota((CHUNK, CHUNK), 0) >= _iota((CHUNK, CHUNK), 1)).astype(F32)
    pick = (_iota((CHUNK, GROUP_W), 1) % HEAD_DIM == 0).astype(F32)
    return dict(r=r, c=c, same_head=same_head, incl=incl, strict=strict, ltri=ltri, pick=pick)


def _seg_ones(width, seg):
    r = _iota((width, width), 0)
    c = _iota((width, width), 1)
    return ((r // seg) == (c // seg)).astype(F32)


def _row_form(colvals, m):
    rows = jnp.where(m["same_head"], _tile_rows(colvals, N_HEADS), 0.0)
    return _mm_sel_l(m["pick"], rows, "nt")


def _to_bd(cat, m):
    return jnp.where(m["same_head"], _tile_rows(cat, N_HEADS), 0.0)


def _rows_masked(x, m):
    return jnp.where(m["same_head"], _tile_rows(x, N_HEADS), 0.0)


def _collapse(rows):
    out = rows[0:CHUNK]
    for h in range(1, N_HEADS):
        out = out + rows[h * CHUNK:(h + 1) * CHUNK]
    return out


def _inv_unit_lower(n_bd, m):
    r, c = m["r"], m["c"]
    eye = (r == c).astype(F32)
    d = jnp.where((r // 8) == (c // 8), n_bd, 0.0)
    d2 = _mm3(d, d)
    d4 = _mm3(d2, d2)
    t = _mm3(eye - d, eye + d2)
    t = _mm3(t, eye + d4)
    for s in (8, 16, 32):
        off = ((r // (2 * s)) == (c // (2 * s))) & (((r // s) % 2) == 1) & (((c // s) % 2) == 0)
        o = jnp.where(off, n_bd, 0.0)
        t = t - _mm3(_mm3(t, o), t)
    return t


def _causal_conv(cur, halo, w_ref, first):
    halo = jnp.where(first, 0.0, halo)
    ext = jnp.concatenate([halo, cur], axis=0)
    acc = cur * w_ref[CONV_K - 1:CONV_K, :]
    for j in range(CONV_K - 1):
        sh = pltpu.roll(ext, CONV_K - 1 - j, axis=0)[HALO:]
        acc = acc + sh * w_ref[j:j + 1, :]
    return acc


def _ffn_kernel(h_ref, g_ref, wg_ref, wu_ref, wd_ref, o_ref):
    x = h_ref[...]
    u = _rms(x, g_ref[...]).astype(BF16)
    a = jnp.dot(u, wg_ref[...], preferred_element_type=F32)
    b = jnp.dot(u, wu_ref[...], preferred_element_type=F32)
    act = (_silu(a) * b).astype(BF16)
    y = jnp.dot(act, wd_ref[...], preferred_element_type=F32)
    o_ref[...] = x + 0.5 * y


def _const_spec(shape):
    nd = len(shape)
    return pl.BlockSpec(shape, lambda *_: (0,) * nd)


def _params(sem):
    return pltpu.CompilerParams(dimension_semantics=sem, vmem_limit_bytes=VMEM_LIMIT)


def _ffn(h, gain, wg, wu, wd, tm):
    n, d = h.shape
    f = wg.shape[1]
    return pl.pallas_call(
        _ffn_kernel,
        grid=(n // tm,),
        in_specs=[pl.BlockSpec((tm, d), lambda i: (i, 0)), _const_spec((1, d)),
                  _const_spec((d, f)), _const_spec((d, f)), _const_spec((f, d))],
        out_specs=pl.BlockSpec((tm, d), lambda i: (i, 0)),
        out_shape=jax.ShapeDtypeStruct((n, d), F32),
        compiler_params=_params(("parallel",)),
        name="ffn",
    )(h, gain, wg, wu, wd)


def _inproj_kernel(h_ref, g_ref, wa_ref, wb_ref, wc_ref, wd_ref, oa_ref, ob_ref, oc_ref, od_ref):
    u = _rms(h_ref[...], g_ref[...]).astype(BF16)
    for w_ref, o_ref in ((wa_ref, oa_ref), (wb_ref, ob_ref), (wc_ref, oc_ref), (wd_ref, od_ref)):
        o_ref[...] = jnp.dot(u, w_ref[...], preferred_element_type=F32).astype(o_ref.dtype)


def _inproj(h, gain, ws, tm):
    n, d = h.shape
    widths = [w.shape[1] for w in ws]
    return pl.pallas_call(
        _inproj_kernel,
        grid=(n // tm,),
        in_specs=[pl.BlockSpec((tm, d), lambda i: (i, 0)), _const_spec((1, d))]
                 + [_const_spec((d, wd)) for wd in widths],
        out_specs=[pl.BlockSpec((tm, wd), lambda i: (i, 0)) for wd in widths],
        out_shape=[jax.ShapeDtypeStruct((n, wd), F32) for wd in widths],
        compiler_params=_params(("parallel",)),
        name="inproj",
    )(h, gain, *ws)


def _outproj_kernel(h_ref, ya_ref, yb_ref, yc_ref, yd_ref, w_ref, o_ref):
    acc = h_ref[...]
    for i, y_ref in enumerate((ya_ref, yb_ref, yc_ref, yd_ref)):
        acc = acc + jnp.dot(y_ref[...].astype(BF16), w_ref[i * GROUP_W:(i + 1) * GROUP_W, :],
                            preferred_element_type=F32)
    o_ref[...] = acc


def _outproj(h, ys, w, tm):
    n, d = h.shape
    return pl.pallas_call(
        _outproj_kernel,
        grid=(n // tm,),
        in_specs=[pl.BlockSpec((tm, d), lambda i: (i, 0))]
                 + [pl.BlockSpec((tm, GROUP_W), lambda i: (i, 0)) for _ in ys]
                 + [_const_spec(w.shape)],
        out_specs=pl.BlockSpec((tm, d), lambda i: (i, 0)),
        out_shape=jax.ShapeDtypeStruct((n, d), F32),
        compiler_params=_params(("parallel",)),
        name="outproj",
    )(h, *ys, w)


def _ple_kernel(h_ref, p_ref, g_ref, wg_ref, wp_ref, fg_ref, o_ref, *, final):
    x = h_ref[...]
    u = _rms(x, g_ref[...]).astype(BF16)
    gate = _sigmoid(jnp.dot(u, wg_ref[...], preferred_element_type=F32))
    e = jnp.dot(p_ref[...].astype(BF16), wp_ref[...], preferred_element_type=F32)
    y = x + e * gate
    if final:
        y = _rms(y, fg_ref[...])
    o_ref[...] = y


def _ple(h, p, gain, wg, wp, fgain, final, tm):
    n, d = h.shape
    pd = p.shape[1]
    return pl.pallas_call(
        functools.partial(_ple_kernel, final=final),
        grid=(n // tm,),
        in_specs=[pl.BlockSpec((tm, d), lambda i: (i, 0)), pl.BlockSpec((tm, pd), lambda i: (i, 0)),
                  _const_spec((1, d)), _const_spec((d, d)), _const_spec((pd, d)), _const_spec((1, d))],
        out_specs=pl.BlockSpec((tm, d), lambda i: (i, 0)),
        out_shape=jax.ShapeDtypeStruct((n, d), F32),
        compiler_params=_params(("parallel",)),
        name="ple",
    )(h, p, gain, wg, wp, fgain)


SSD_W = 1152


def _ssd_kernel(x_ref, halo_ref, cw_ref, cb_ref, dtb_ref, alog_ref, dsk_ref, nw_ref, o_ref, st_ref):
    ci = pl.program_id(1)

    @pl.when(ci == 0)
    def _():
        st_ref[...] = jnp.zeros_like(st_ref)

    m = _chunk_masks()
    x = x_ref[0]
    z = x[:, 0:GROUP_W]
    xbc = _causal_conv(x[:, GROUP_W:4 * GROUP_W], halo_ref[0][:, GROUP_W:4 * GROUP_W], cw_ref, ci == 0)
    xbc = _silu(xbc + cb_ref[...])
    xs = xbc[:, 0:GROUP_W]
    bm = xbc[:, GROUP_W:2 * GROUP_W]
    cm = xbc[:, 2 * GROUP_W:3 * GROUP_W]
    dt_pad = _softplus(x[:, 4 * GROUP_W:] + dtb_ref[...])
    la_pad = dt_pad * (-jnp.exp(alog_ref[...]))
    expand = (_iota((LANE, GROUP_W), 0) == _iota((LANE, GROUP_W), 1) // HEAD_DIM).astype(F32)
    dt_b = _mm_sel_r(dt_pad, expand)
    la_b = _mm_sel_r(la_pad, expand)
    cs = _mm_sel_l(m["ltri"], la_b)
    xc = xs * dt_b
    seg = jnp.exp(jnp.where(m["incl"], cs - _row_form(cs, m), -jnp.inf))
    grp_rows = (m["r"] // HEAD_DIM) // 2 == m["c"] // SSM_STATE
    b_rows = jnp.where(grp_rows, _tile_rows(bm, N_HEADS), 0.0)
    scores = _mm3(cm, b_rows, "nt") * seg
    y = _mm3(scores, _rows_masked(xc, m))
    grp_state = m["r"] // SSM_STATE == (m["c"] // HEAD_DIM) // 2
    st = st_ref[...]
    y = y + _mm3(cm, st) * jnp.exp(cs)
    last = cs[CHUNK - 1:CHUNK, :]
    to_end = jnp.exp(last - cs)
    st_ref[...] = st * jnp.exp(last) + jnp.where(grp_state, _mm3(bm, xc * to_end, "tn"), 0.0)
    y = (y + dsk_ref[...] * xs) * _silu(z)
    ms = _mm_sel_r(y * y, _seg_ones(GROUP_W, SSM_STATE)) * (1.0 / SSM_STATE)
    o_ref[0] = y * lax.rsqrt(ms + NORM_EPS) * nw_ref[...]


def _halo_map(rows_per_chunk):
    step = rows_per_chunk // HALO
    return lambda b, i: (b, jnp.maximum(i * step - 1, 0), 0)


def _ssd(pa, cw, cb, dtb, alog, dsk, nw):
    bsz, s, w = pa.shape
    return pl.pallas_call(
        _ssd_kernel,
        grid=(bsz, s // CHUNK),
        in_specs=[pl.BlockSpec((1, CHUNK, w), lambda b, i: (b, i, 0)),
                  pl.BlockSpec((1, HALO, w), _halo_map(CHUNK)),
                  _const_spec(cw.shape), _const_spec(cb.shape), _const_spec(dtb.shape),
                  _const_spec(alog.shape), _const_spec(dsk.shape), _const_spec(nw.shape)],
        out_specs=pl.BlockSpec((1, CHUNK, GROUP_W), lambda b, i: (b, i, 0)),
        out_shape=jax.ShapeDtypeStruct((bsz, s, GROUP_W), F32),
        scratch_shapes=[pltpu.VMEM((GROUP_W, GROUP_W), F32)],
        compiler_params=_params(("parallel", "arbitrary")),
        name="ssd",
    )(pa, pa, cw, cb, dtb, alog, dsk, nw)


GDN_W = 1152


def _gdn_kernel(x_ref, halo_ref, cw_ref, alog_ref, dtb_ref, nw_ref, o_ref, st_ref):
    ci = pl.program_id(1)

    @pl.when(ci == 0)
    def _():
        st_ref[...] = jnp.zeros_like(st_ref)

    m = _chunk_masks()
    ones_h = _seg_ones(GROUP_W, HEAD_DIM)
    x = x_ref[0]
    qkv = _silu(_causal_conv(x[:, 0:3 * GROUP_W], halo_ref[0][:, 0:3 * GROUP_W], cw_ref, ci == 0))
    z = x[:, 3 * GROUP_W:4 * GROUP_W]
    ba = x[:, 4 * GROUP_W:]
    q = qkv[:, 0:GROUP_W]
    k = qkv[:, GROUP_W:2 * GROUP_W]
    v = qkv[:, 2 * GROUP_W:3 * GROUP_W]
    q = q * lax.rsqrt(_mm_sel_r(q * q, ones_h) + L2_EPS) * (HEAD_DIM ** -0.5)
    k = k * lax.rsqrt(_mm_sel_r(k * k, ones_h) + L2_EPS)
    beta_pad = _sigmoid(ba)
    g_pad = -jnp.exp(alog_ref[...]) * _softplus(ba + dtb_ref[...])
    er = _iota((LANE, GROUP_W), 0)
    ec = _iota((LANE, GROUP_W), 1) // HEAD_DIM
    beta_b = _mm_sel_r(beta_pad, (er == ec).astype(F32))
    g_b = _mm_sel_r(g_pad, (er == ec + N_HEADS).astype(F32))
    gc = _mm_sel_l(m["ltri"], g_b)
    eg = jnp.exp(gc)
    kb = k * beta_b
    vb = v * beta_b
    decay = jnp.exp(jnp.where(m["incl"], gc - _row_form(gc, m), -jnp.inf))
    k_rows = _rows_masked(k, m)
    a_cat = jnp.where(m["strict"], _mm3(kb, k_rows, "nt") * decay, 0.0)
    qk_cat = _mm3(q, k_rows, "nt") * decay
    t_bd = _inv_unit_lower(_to_bd(a_cat, m), m)
    u = _collapse(_mm3(t_bd, _rows_masked(vb, m)))
    w = _collapse(_mm3(t_bd, _rows_masked(kb * eg, m)))
    st = st_ref[...]
    v_new = u - _mm3(w, st)
    o = _mm3(q * eg, st) + _mm3(qk_cat, _rows_masked(v_new, m))
    last = gc[CHUNK - 1:CHUNK, :]
    k_dec = k * jnp.exp(last - gc)
    st_ref[...] = st * jnp.exp(last) + jnp.where(m["same_head"], _mm3(k_dec, v_new, "tn"), 0.0)
    ms = _mm_sel_r(o * o, ones_h) * (1.0 / HEAD_DIM)
    o_ref[0] = o * lax.rsqrt(ms + NORM_EPS) * nw_ref[...] * _silu(z)


def _gdn(pc, cw, alog, dtb, nw):
    bsz, s, w = pc.shape
    return pl.pallas_call(
        _gdn_kernel,
        grid=(bsz, s // CHUNK),
        in_specs=[pl.BlockSpec((1, CHUNK, w), lambda b, i: (b, i, 0)),
                  pl.BlockSpec((1, HALO, w), _halo_map(CHUNK)),
                  _const_spec(cw.shape), _const_spec(alog.shape), _const_spec(dtb.shape),
                  _const_spec(nw.shape)],
        out_specs=pl.BlockSpec((1, CHUNK, GROUP_W), lambda b, i: (b, i, 0)),
        out_shape=jax.ShapeDtypeStruct((bsz, s, GROUP_W), F32),
        scratch_shapes=[pltpu.VMEM((GROUP_W, GROUP_W), F32)],
        compiler_params=_params(("parallel", "arbitrary")),
        name="gdn",
    )(pc, pc, cw, alog, dtb, nw)


RWKV_W = 896


def _rwkv_kernel(x_ref, halo_ref, mu_ref, w0_ref, wup_ref, a0_ref, aup_ref, gup_ref, kk_ref, ka_ref,
                 rk_ref, lnw_ref, lnb_ref, o_ref, st_ref):
    ci = pl.program_id(1)

    @pl.when(ci == 0)
    def _():
        st_ref[...] = jnp.zeros_like(st_ref)

    m = _chunk_masks()
    ones_h = _seg_ones(GROUP_W, HEAD_DIM)
    x = x_ref[0]
    halo = jnp.where(ci == 0, 0.0, halo_ref[0])
    shifted = pltpu.roll(jnp.concatenate([halo, x], axis=0), 1, axis=0)[HALO:]
    x = x + mu_ref[...] * (shifted - x)
    r = x[:, 0:GROUP_W]
    k = x[:, GROUP_W:2 * GROUP_W]
    v = x[:, 2 * GROUP_W:3 * GROUP_W]
    lora = x[:, 3 * GROUP_W:]
    w_log = -_softplus(-(w0_ref[...] + _mm3(jnp.tanh(lora), wup_ref[...]))) - RWKV_DECAY_OFFSET
    lw = -jnp.exp(w_log)
    a_gate = _sigmoid(a0_ref[...] + _mm3(lora, aup_ref[...]))
    g = _mm3(_sigmoid(lora), gup_ref[...])
    kk = k * kk_ref[...]
    kk = kk * lax.rsqrt(_mm_sel_r(kk * kk, ones_h) + L2_EPS)
    k = k * (1.0 + (a_gate - 1.0) * ka_ref[...])
    a_v = -kk
    b_v = kk * a_gate
    cl = _mm_sel_l(m["ltri"], lw)
    e_in = jnp.exp(cl)
    e_inv = jnp.exp(-cl)
    a_t = a_v * jnp.exp(cl - lw)
    r_t = r * e_in
    b_rows = _rows_masked(b_v * e_inv, m)
    k_rows = _rows_masked(k * e_inv, m)
    v_rows = _rows_masked(v, m)
    zero = jnp.zeros((), F32)
    n_cat = jnp.where(m["strict"], -_mm3(a_t, b_rows, "nt"), zero)
    ak_cat = jnp.where(m["strict"], _mm3(a_t, k_rows, "nt"), zero)
    rb_cat = jnp.where(m["incl"], _mm3(r_t, b_rows, "nt"), zero)
    rk_cat = jnp.where(m["incl"], _mm3(r_t, k_rows, "nt"), zero)
    st = st_ref[...]
    rhs = _mm3(a_t, st, "nt") + _mm3(ak_cat, v_rows)
    t_bd = _inv_unit_lower(_to_bd(n_cat, m), m)
    u_rows = _mm3(t_bd, _rows_masked(rhs, m))
    y = _mm3(r_t, st, "nt") + _mm3(rb_cat, u_rows) + _mm3(rk_cat, v_rows)
    u = _collapse(u_rows)
    last = cl[CHUNK - 1:CHUNK, :]
    to_end = jnp.exp(last - cl)
    upd = _mm3(u, b_v * to_end, "tn") + _mm3(v, k * to_end, "tn")
    st_ref[...] = st * jnp.exp(last) + jnp.where(m["same_head"], upd, 0.0)
    inv_d = 1.0 / HEAD_DIM
    mean = _mm_sel_r(y, ones_h) * inv_d
    yc = y - mean
    var = _mm_sel_r(yc * yc, ones_h) * inv_d
    yn = yc * lax.rsqrt(var + RWKV_GN_EPS) * lnw_ref[...] + lnb_ref[...]
    bonus = _mm_sel_r(r * k * rk_ref[...], ones_h) * v
    o_ref[0] = (yn + bonus) * g


def _rwkv(pb, *consts):
    bsz, s, w = pb.shape
    return pl.pallas_call(
        _rwkv_kernel,
        grid=(bsz, s // CHUNK),
        in_specs=[pl.BlockSpec((1, CHUNK, w), lambda b, i: (b, i, 0)),
                  pl.BlockSpec((1, HALO, w), _halo_map(CHUNK))]
                 + [_const_spec(c.shape) for c in consts],
        out_specs=pl.BlockSpec((1, CHUNK, GROUP_W), lambda b, i: (b, i, 0)),
        out_shape=jax.ShapeDtypeStruct((bsz, s, GROUP_W), F32),
        scratch_shapes=[pltpu.VMEM((GROUP_W, GROUP_W), F32)],
        compiler_params=_params(("parallel", "arbitrary")),
        name="rwkv",
    )(pb, pb, *consts)


SB_BLOCK = 256


def _sb_kernel(q_ref, k_ref, v_ref, o_ref):
    qi = pl.program_id(2)
    t = SB_BLOCK
    q = q_ref[0, 0]
    scale = HEAD_DIM ** -0.5
    upper = (_iota((t, t), 0) > _iota((t, t), 1)).astype(BF16)
    causal = _iota((t, t), 1) < _iota((t, t), 0)

    def block(j, acc, run, masked):
        start = pl.multiple_of(j * t, t)
        kj = k_ref[0, 0, pl.ds(start, t), :]
        vj = v_ref[0, 0, pl.ds(start, t), :]
        zz = _dg(q, kj, 1, 1) * scale
        sp = _softplus(zz)
        l1m = -sp
        if masked:
            l1m = jnp.where(causal, l1m, 0.0)
        hi, lo = _split2(l1m)
        suffix = _dg(hi, upper, 1, 0) + _dg(lo, upper, 1, 0)
        wgt = jnp.exp(zz - sp + suffix + run)
        if masked:
            wgt = jnp.where(causal, wgt, 0.0)
        acc = acc + _dg(wgt.astype(BF16), vj, 1, 0)
        run = run + jnp.sum(l1m, axis=1, keepdims=True)
        return acc, run

    acc, run = block(qi, jnp.zeros((t, HEAD_DIM), F32), jnp.zeros((t, 1), F32), True)

    def body(jj, carry):
        return block(qi - 1 - jj, carry[0], carry[1], False)

    acc, run = lax.fori_loop(0, qi, body, (acc, run))
    o_ref[0, 0] = acc


def _sb(q, k, v):
    bsz, nh, s, d = q.shape
    t = SB_BLOCK
    return pl.pallas_call(
        _sb_kernel,
        grid=(bsz, nh, s // t),
        in_specs=[pl.BlockSpec((1, 1, t, d), lambda b, h, i: (b, h, i, 0)),
                  pl.BlockSpec((1, 1, s, d), lambda b, h, i: (b, h, 0, 0)),
                  pl.BlockSpec((1, 1, s, d), lambda b, h, i: (b, h, 0, 0))],
        out_specs=pl.BlockSpec((1, 1, t, d), lambda b, h, i: (b, h, i, 0)),
        out_shape=jax.ShapeDtypeStruct((bsz, nh, s, d), F32),
        compiler_params=_params(("parallel", "parallel", "arbitrary")),
        name="stickbreak",
    )(q, k, v)


def _row(v, width=None):
    v = v.reshape(1, -1).astype(F32)
    if width is not None and v.shape[1] < width:
        v = jnp.pad(v, ((0, 0), (0, width - v.shape[1])))
    return v


def _pad_cols(w, width):
    return jnp.pad(w, ((0, 0), (0, width - w.shape[1])))


def _pad_rows_at(w, offset, total):
    return jnp.pad(w, ((offset, total - offset - w.shape[0]), (0, 0)))


def _mixer(h2, bsz, s, gain, w_in, ssm_conv_w, ssm_conv_b, ssm_dt_bias, ssm_a_log, ssm_d, ssm_norm,
           rwkv_mu, rwkv_w0, rwkv_w_up, rwkv_a0, rwkv_a_up, rwkv_g_up, rwkv_k_k, rwkv_k_a, rwkv_r_k,
           rwkv_ln_w, rwkv_ln_b, gdn_conv_w, gdn_a_log, gdn_dt_bias, gdn_norm, w_out, tm):
    gw = GROUP_W
    ssm_in = 4 * gw + N_HEADS
    c0 = ssm_in
    c1 = c0 + RWKV_W
    c2 = c1 + 4 * gw + 2 * N_HEADS
    wa = _pad_cols(w_in[:, :c0], SSD_W).astype(BF16)
    wb = w_in[:, c0:c1].astype(BF16)
    wc = _pad_cols(w_in[:, c1:c2], GDN_W).astype(BF16)
    wd = w_in[:, c2:].astype(BF16)
    pa, pb, pc, pd = _inproj(h2, _row(gain), [wa, wb, wc, wd], tm)

    rep = HEAD_DIM
    ya = _ssd(pa.reshape(bsz, s, SSD_W), ssm_conv_w.astype(F32), _row(ssm_conv_b),
              _row(ssm_dt_bias, LANE), _row(ssm_a_log, LANE), _row(jnp.repeat(ssm_d, rep)),
              _row(ssm_norm))

    lora_w = RWKV_W - 3 * gw
    yb = _rwkv(pb.reshape(bsz, s, RWKV_W), _row(rwkv_mu), _row(rwkv_w0),
               _pad_rows_at(rwkv_w_up, 0, lora_w), _row(rwkv_a0),
               _pad_rows_at(rwkv_a_up, rwkv_w_up.shape[0], lora_w),
               _pad_rows_at(rwkv_g_up, rwkv_w_up.shape[0] + rwkv_a_up.shape[0], lora_w),
               _row(rwkv_k_k), _row(rwkv_k_a), _row(rwkv_r_k), _row(rwkv_ln_w), _row(rwkv_ln_b))

    pad4 = lambda t: jnp.pad(t.reshape(1, -1).astype(F32), ((0, 0), (N_HEADS, LANE - 2 * N_HEADS)))
    yc = _gdn(pc.reshape(bsz, s, GDN_W), gdn_conv_w.astype(F32), pad4(gdn_a_log), pad4(gdn_dt_bias),
              _row(jnp.tile(gdn_norm, N_HEADS)))

    qkv = pd.reshape(bsz, s, 3, N_HEADS, HEAD_DIM).astype(BF16)
    qkv = jnp.transpose(qkv, (2, 0, 3, 1, 4))
    yd = _sb(qkv[0], qkv[1], qkv[2])
    yd = jnp.transpose(yd, (0, 2, 1, 3)).reshape(bsz * s, gw)

    n = bsz * s
    return _outproj(h2, [ya.reshape(n, gw), yb.reshape(n, gw), yc.reshape(n, gw), yd],
                    w_out.astype(BF16), tm)


def kernel(x, p, ffn1_norm, ffn1_w_gate, ffn1_w_up, ffn1_w_down, mix_norm, w_in, ssm_conv_w, ssm_conv_b, ssm_dt_bias, ssm_a_log, ssm_d, ssm_norm, rwkv_mu, rwkv_w0, rwkv_w_up, rwkv_a0, rwkv_a_up, rwkv_g_up, rwkv_k_k, rwkv_k_a, rwkv_r_k, rwkv_ln_w, rwkv_ln_b, gdn_conv_w, gdn_a_log, gdn_dt_bias, gdn_norm, w_out, ffn2_norm, ffn2_w_gate, ffn2_w_up, ffn2_w_down, ple_norm, ple_w_gate, ple_w_proj, final_norm):
    bsz, s, d = x.shape
    depth = p.shape[0]
    n = bsz * s
    tm = min(512, n)
    h = x.reshape(n, d)
    for i in range(depth):
        h = _ffn(h, _row(ffn1_norm[i]), ffn1_w_gate[i].astype(BF16), ffn1_w_up[i].astype(BF16),
                 ffn1_w_down[i].astype(BF16), tm)
        h = _mixer(h, bsz, s, mix_norm[i], w_in[i], ssm_conv_w[i], ssm_conv_b[i], ssm_dt_bias[i],
                   ssm_a_log[i], ssm_d[i], ssm_norm[i], rwkv_mu[i], rwkv_w0[i], rwkv_w_up[i],
                   rwkv_a0[i], rwkv_a_up[i], rwkv_g_up[i], rwkv_k_k[i], rwkv_k_a[i], rwkv_r_k[i],
                   rwkv_ln_w[i], rwkv_ln_b[i], gdn_conv_w[i], gdn_a_log[i], gdn_dt_bias[i],
                   gdn_norm[i], w_out[i], tm)
        h = _ffn(h, _row(ffn2_norm[i]), ffn2_w_gate[i].astype(BF16), ffn2_w_up[i].astype(BF16),
                 ffn2_w_down[i].astype(BF16), tm)
        h = _ple(h, p[i].reshape(n, -1), _row(ple_norm[i]), ple_w_gate[i].astype(BF16),
                 ple_w_proj[i].astype(BF16), _row(final_norm), i == depth - 1, tm)
    return h.reshape(bsz, s, d)
```

```python
import functools

import numpy as np
import jax
import jax.numpy as jnp
from jax import lax
from jax.experimental import pallas as pl
from jax.experimental.pallas import tpu as pltpu

F32 = jnp.float32
BF16 = jnp.bfloat16

HEAD_DIM = 64
N_HEADS = 4
GROUP_W = HEAD_DIM * N_HEADS
CHUNK = 64
N_CHUNKS = 4
CONV_K = 4
SSM_STATE = 128
NORM_EPS = 1e-6
L2_EPS = 1e-6
RWKV_GN_EPS = 64e-5
RWKV_DECAY_OFFSET = 0.5
HALO = 8
LANE = 128
VMEM_LIMIT = 56 * 1024 * 1024


def _dg(a, b, ca, cb):
    return lax.dot_general(a, b, (((ca,), (cb,)), ((), ())), preferred_element_type=F32)


_DIMS = {"nn": (1, 0), "nt": (1, 1), "tn": (0, 0)}


def _mm(a, b, kind="nn"):
    ca, cb = _DIMS[kind]
    return _dg(a.astype(BF16), b.astype(BF16), ca, cb)


def _split2(x):
    hi = x.astype(BF16)
    lo = (x - hi.astype(F32)).astype(BF16)
    return hi, lo


def _split3(x):
    hi = x.astype(BF16)
    r = x - hi.astype(F32)
    mid = r.astype(BF16)
    lo = (r - mid.astype(F32)).astype(BF16)
    return hi, mid, lo


def _mm_sel_l(sel, b, kind="nn"):
    ca, cb = _DIMS[kind]
    h, m, l = _split3(b)
    return _dg(sel, h, ca, cb) + (_dg(sel, m, ca, cb) + _dg(sel, l, ca, cb))


def _mm_sel_r(a, sel, kind="nn"):
    ca, cb = _DIMS[kind]
    h, m, l = _split3(a)
    return _dg(h, sel, ca, cb) + (_dg(m, sel, ca, cb) + _dg(l, sel, ca, cb))


def _mm_sel_r2(a, sel):
    h, l = _split2(a)
    return _dg(h, sel, 1, 0) + _dg(l, sel, 1, 0)


def _sigmoid(x):
    return 1.0 / (1.0 + jnp.exp(-x))


def _silu(x):
    return x * _sigmoid(x)


def _softplus(x):
    return jnp.maximum(x, 0.0) + jnp.log1p(jnp.exp(-jnp.abs(x)))


def _iota(shape, dim):
    return lax.broadcasted_iota(jnp.int32, shape, dim)


def _tile_rows(x, n):
    return jnp.concatenate([x] * n, axis=0)


def _rms(x, gain_row):
    ms = jnp.mean(x * x, axis=-1, keepdims=True)
    return x * lax.rsqrt(ms + NORM_EPS) * gain_row


BDF_SAME, BDF_GRP_STATE = range(2)
BDB_SAME, BDB_EYE, BDB_M8, BDB_OFF8, BDB_OFF16, BDB_OFF32, BDB_GRP_ROWS = range(7)
CAT_NEG, CAT_STRICT, CAT_INCL = range(3)
SEL_ONES_H, SEL_ONES_G, SEL_LTRI = range(3)


def _np_consts():
    r = np.arange(GROUP_W)[:, None]
    c = np.arange(GROUP_W)[None, :]
    same = (r // HEAD_DIM) == (c // HEAD_DIM)

    def off(s):
        return ((r // (2 * s)) == (c // (2 * s))) & (((r // s) % 2) == 1) & (((c // s) % 2) == 0)

    bdf = np.stack([same, (r // SSM_STATE) == ((c // HEAD_DIM) // 2)]).astype(np.float32)
    bdb = np.stack([same, r == c, (r // 8) == (c // 8), off(8), off(16), off(32),
                    ((r // HEAD_DIM) // 2) == (c // SSM_STATE)]).astype(np.float32)
    l = np.arange(CHUNK)[:, None]
    s = np.arange(GROUP_W)[None, :] % CHUNK
    cat = np.stack([np.where(l >= s, 0.0, -np.inf), l > s, l >= s]).astype(np.float32)
    t = np.arange(N_CHUNKS * CHUNK)
    ltri = (t[:, None] >= t[None, :]) & ((t[:, None] // CHUNK) == (t[None, :] // CHUNK))
    sel = np.stack([same, (r // SSM_STATE) == (c // SSM_STATE), ltri]).astype(np.float32)
    er = np.arange(LANE)[:, None]
    ec = np.arange(GROUP_W)[None, :] // HEAD_DIM
    expand = np.stack([er == ec, er == ec + N_HEADS]).astype(np.float32)
    pick = np.broadcast_to(np.arange(GROUP_W)[None, :] % HEAD_DIM == 0, (CHUNK, GROUP_W)).astype(np.float32)
    return (jnp.asarray(bdf), jnp.asarray(bdb, BF16), jnp.asarray(cat), jnp.asarray(sel, BF16),
            jnp.asarray(expand, BF16), jnp.asarray(pick, BF16))


def _rows(x_bf, mask_bf):
    return _tile_rows(x_bf, N_HEADS) * mask_bf


def _row_form(colvals, bdb_ref, pick_ref):
    same = bdb_ref[BDB_SAME]
    pick = pick_ref[...]
    parts = [_split3(x) for x in colvals]
    prods = [[_dg(pick, _rows(p, same), 1, 1) for p in ps] for ps in parts]
    return [h + (m + l) for h, m, l in prods]


def _collapse(rows):
    out = rows[0:CHUNK]
    for h in range(1, N_HEADS):
        out = out + rows[h * CHUNK:(h + 1) * CHUNK]
    return out


def _inv_unit_lower(n_cats_bf, bdb_ref):
    tiled = [_tile_rows(n, N_HEADS) for n in n_cats_bf]
    eye = bdb_ref[BDB_EYE]
    m8 = bdb_ref[BDB_M8]
    d = [x * m8 for x in tiled]
    d2 = [_dg(x, x, 1, 0).astype(BF16) for x in d]
    d4 = [_dg(x, x, 1, 0).astype(BF16) for x in d2]
    t = [_dg(eye - x, eye + y, 1, 0) for x, y in zip(d, d2)]
    t = [_dg(x.astype(BF16), eye + y, 1, 0) for x, y in zip(t, d4)]
    for idx in (BDB_OFF8, BDB_OFF16, BDB_OFF32):
        off = bdb_ref[idx]
        tb = [x.astype(BF16) for x in t]
        to = [_dg(x, y * off, 1, 0).astype(BF16) for x, y in zip(tb, tiled)]
        t = [x - _dg(y, z, 1, 0) for x, y, z in zip(t, to, tb)]
    return [x.astype(BF16) for x in t]


def _causal_conv(cur, halo, w_ref, first):
    halo = jnp.where(first, 0.0, halo)
    ext = jnp.concatenate([halo, cur], axis=0)
    acc = cur * w_ref[CONV_K - 1:CONV_K, :]
    for j in range(CONV_K - 1):
        sh = pltpu.roll(ext, CONV_K - 1 - j, axis=0)[HALO:]
        acc = acc + sh * w_ref[j:j + 1, :]
    return acc


def _chunk_rows(c):
    return slice(c * CHUNK, (c + 1) * CHUNK)


def _ffn_kernel(h_ref, g_ref, wg_ref, wu_ref, wd_ref, o_ref):
    x = h_ref[...]
    u = _rms(x, g_ref[...]).astype(BF16)
    a = jnp.dot(u, wg_ref[...], preferred_element_type=F32)
    b = jnp.dot(u, wu_ref[...], preferred_element_type=F32)
    act = (_silu(a) * b).astype(BF16)
    y = jnp.dot(act, wd_ref[...], preferred_element_type=F32)
    o_ref[...] = x + 0.5 * y


def _const_spec(shape):
    nd = len(shape)
    return pl.BlockSpec(shape, lambda *_: (0,) * nd)


def _params(sem):
    return pltpu.CompilerParams(dimension_semantics=sem, vmem_limit_bytes=VMEM_LIMIT)


def _ffn(h, gain, wg, wu, wd, tm):
    n, d = h.shape
    f = wg.shape[1]
    return pl.pallas_call(
        _ffn_kernel,
        grid=(n // tm,),
        in_specs=[pl.BlockSpec((tm, d), lambda i: (i, 0)), _const_spec((1, d)),
                  _const_spec((d, f)), _const_spec((d, f)), _const_spec((f, d))],
        out_specs=pl.BlockSpec((tm, d), lambda i: (i, 0)),
        out_shape=jax.ShapeDtypeStruct((n, d), F32),
        compiler_params=_params(("parallel",)),
        name="ffn",
    )(h, gain, wg, wu, wd)


def _inproj_kernel(h_ref, g_ref, wa_ref, wb_ref, wc_ref, wd_ref, oa_ref, ob_ref, oc_ref, od_ref):
    u = _rms(h_ref[...], g_ref[...]).astype(BF16)
    for w_ref, o_ref in ((wa_ref, oa_ref), (wb_ref, ob_ref), (wc_ref, oc_ref), (wd_ref, od_ref)):
        o_ref[...] = jnp.dot(u, w_ref[...], preferred_element_type=F32).astype(o_ref.dtype)


def _inproj(h, gain, ws, tm):
    n, d = h.shape
    widths = [w.shape[1] for w in ws]
    dtypes = [F32, F32, F32, BF16]
    return pl.pallas_call(
        _inproj_kernel,
        grid=(n // tm,),
        in_specs=[pl.BlockSpec((tm, d), lambda i: (i, 0)), _const_spec((1, d))]
                 + [_const_spec((d, wd)) for wd in widths],
        out_specs=[pl.BlockSpec((tm, wd), lambda i: (i, 0)) for wd in widths],
        out_shape=[jax.ShapeDtypeStruct((n, wd), dt) for wd, dt in zip(widths, dtypes)],
        compiler_params=_params(("parallel",)),
        name="inproj",
    )(h, gain, *ws)


def _outproj_kernel(h_ref, ya_ref, yb_ref, yc_ref, yd_ref, w_ref, o_ref):
    acc = h_ref[...]
    for i, y_ref in enumerate((ya_ref, yb_ref, yc_ref, yd_ref)):
        acc = acc + jnp.dot(y_ref[...].astype(BF16), w_ref[i * GROUP_W:(i + 1) * GROUP_W, :],
                            preferred_element_type=F32)
    o_ref[...] = acc


def _outproj(h, ys, w, tm):
    n, d = h.shape
    return pl.pallas_call(
        _outproj_kernel,
        grid=(n // tm,),
        in_specs=[pl.BlockSpec((tm, d), lambda i: (i, 0))]
                 + [pl.BlockSpec((tm, GROUP_W), lambda i: (i, 0)) for _ in ys]
                 + [_const_spec(w.shape)],
        out_specs=pl.BlockSpec((tm, d), lambda i: (i, 0)),
        out_shape=jax.ShapeDtypeStruct((n, d), F32),
        compiler_params=_params(("parallel",)),
        name="outproj",
    )(h, *ys, w)


def _ple_kernel(h_ref, p_ref, g_ref, wg_ref, wp_ref, fg_ref, o_ref, *, final):
    x = h_ref[...]
    u = _rms(x, g_ref[...]).astype(BF16)
    gate = _sigmoid(jnp.dot(u, wg_ref[...], preferred_element_type=F32))
    e = jnp.dot(p_ref[...].astype(BF16), wp_ref[...], preferred_element_type=F32)
    y = x + e * gate
    if final:
        y = _rms(y, fg_ref[...])
    o_ref[...] = y


def _ple(h, p, gain, wg, wp, fgain, final, tm):
    n, d = h.shape
    pd = p.shape[1]
    return pl.pallas_call(
        functools.partial(_ple_kernel, final=final),
        grid=(n // tm,),
        in_specs=[pl.BlockSpec((tm, d), lambda i: (i, 0)), pl.BlockSpec((tm, pd), lambda i: (i, 0)),
                  _const_spec((1, d)), _const_spec((d, d)), _const_spec((pd, d)), _const_spec((1, d))],
        out_specs=pl.BlockSpec((tm, d), lambda i: (i, 0)),
        out_shape=jax.ShapeDtypeStruct((n, d), F32),
        compiler_params=_params(("parallel",)),
        name="ple",
    )(h, p, gain, wg, wp, fgain)


def _halo_map(rows_per_block):
    step = rows_per_block // HALO
    return lambda b, i: (b, jnp.maximum(i * step - 1, 0), 0)


def _mixer_call(kernel_fn, name, x, params):
    bsz, s, w = x.shape
    t = N_CHUNKS * CHUNK
    consts = _np_consts()
    return pl.pallas_call(
        kernel_fn,
        grid=(bsz, s // t),
        in_specs=[pl.BlockSpec((1, t, w), lambda b, i: (b, i, 0)),
                  pl.BlockSpec((1, HALO, w), _halo_map(t))]
                 + [_const_spec(c.shape) for c in consts]
                 + [_const_spec(p.shape) for p in params],
        out_specs=pl.BlockSpec((1, t, GROUP_W), lambda b, i: (b, i, 0)),
        out_shape=jax.ShapeDtypeStruct((bsz, s, GROUP_W), F32),
        scratch_shapes=[pltpu.VMEM((GROUP_W, GROUP_W), F32)],
        compiler_params=_params(("parallel", "arbitrary")),
        name=name,
    )(x, x, *consts, *params)


def _reset_state(st_ref):
    @pl.when(pl.program_id(1) == 0)
    def _():
        st_ref[...] = jnp.zeros_like(st_ref)


SSD_W = 1152


def _ssd_kernel(x_ref, halo_ref, bdf_ref, bdb_ref, cat_ref, sel_ref, exp_ref, pick_ref,
                cw_ref, cb_ref, dtb_ref, alog_ref, dsk_ref, nw_ref, o_ref, st_ref):
    _reset_state(st_ref)
    first = pl.program_id(1) == 0
    x = x_ref[0]
    z = x[:, 0:GROUP_W]
    xbc = _causal_conv(x[:, GROUP_W:4 * GROUP_W], halo_ref[0][:, GROUP_W:4 * GROUP_W], cw_ref, first)
    xbc = _silu(xbc + cb_ref[...])
    xs = xbc[:, 0:GROUP_W]
    bm = xbc[:, GROUP_W:2 * GROUP_W].astype(BF16)
    cm = xbc[:, 2 * GROUP_W:3 * GROUP_W].astype(BF16)
    dt_pad = _softplus(x[:, 4 * GROUP_W:] + dtb_ref[...])
    la_pad = dt_pad * (-jnp.exp(alog_ref[...]))
    dt_b = _mm_sel_r(dt_pad, exp_ref[0])
    la_b = _mm_sel_r(la_pad, exp_ref[0])
    cs = _mm_sel_l(sel_ref[SEL_LTRI], la_b)
    xc = xs * dt_b
    ecs = jnp.exp(cs)
    same = bdb_ref[BDB_SAME]
    grp_rows = bdb_ref[BDB_GRP_ROWS]
    grp_state = bdf_ref[BDF_GRP_STATE]
    chunks = [_chunk_rows(c) for c in range(N_CHUNKS)]
    neg = cat_ref[CAT_NEG]
    rforms = _row_form([cs[rs] for rs in chunks], bdb_ref, pick_ref)
    segs = [jnp.exp(cs[rs] - rf + neg) for rs, rf in zip(chunks, rforms)]
    scores = [(_dg(cm[rs], _rows(bm[rs], grp_rows), 1, 1) * sg).astype(BF16)
              for rs, sg in zip(chunks, segs)]
    xc_bf = xc.astype(BF16)
    ys = [_dg(sc, _rows(xc_bf[rs], same), 1, 0) for rs, sc in zip(chunks, scores)]
    lasts = [cs[rs][CHUNK - 1:CHUNK, :] for rs in chunks]
    upds = [_dg(bm[rs], (xc[rs] * jnp.exp(last - cs[rs])).astype(BF16), 0, 0) * grp_state
            for rs, last in zip(chunks, lasts)]
    st = st_ref[...]
    for c, rs in enumerate(chunks):
        ys[c] = ys[c] + _dg(cm[rs], st.astype(BF16), 1, 0) * ecs[rs]
        st = st * jnp.exp(lasts[c]) + upds[c]
    st_ref[...] = st
    y = (jnp.concatenate(ys, axis=0) + dsk_ref[...] * xs) * _silu(z)
    ms = _mm_sel_r2(y * y, sel_ref[SEL_ONES_G]) * (1.0 / SSM_STATE)
    o_ref[0] = y * lax.rsqrt(ms + NORM_EPS) * nw_ref[...]


GDN_W = 1152


def _gdn_kernel(x_ref, halo_ref, bdf_ref, bdb_ref, cat_ref, sel_ref, exp_ref, pick_ref,
                cw_ref, alog_ref, dtb_ref, nw_ref, o_ref, st_ref):
    _reset_state(st_ref)
    first = pl.program_id(1) == 0
    ones_h = sel_ref[SEL_ONES_H]
    x = x_ref[0]
    qkv = _silu(_causal_conv(x[:, 0:3 * GROUP_W], halo_ref[0][:, 0:3 * GROUP_W], cw_ref, first))
    z = x[:, 3 * GROUP_W:4 * GROUP_W]
    ba = x[:, 4 * GROUP_W:]
    q = qkv[:, 0:GROUP_W]
    k = qkv[:, GROUP_W:2 * GROUP_W]
    v = qkv[:, 2 * GROUP_W:3 * GROUP_W]
    q = q * lax.rsqrt(_mm_sel_r2(q * q, ones_h) + L2_EPS) * (HEAD_DIM ** -0.5)
    k = k * lax.rsqrt(_mm_sel_r2(k * k, ones_h) + L2_EPS)
    beta_pad = _sigmoid(ba)
    g_pad = -jnp.exp(alog_ref[...]) * _softplus(ba + dtb_ref[...])
    beta_b = _mm_sel_r(beta_pad, exp_ref[0])
    g_b = _mm_sel_r(g_pad, exp_ref[1])
    gc = _mm_sel_l(sel_ref[SEL_LTRI], g_b)
    eg = jnp.exp(gc)
    kb = k * beta_b
    q_bf = q.astype(BF16)
    k_bf = k.astype(BF16)
    kb_bf = kb.astype(BF16)
    vb_bf = (v * beta_b).astype(BF16)
    kbg_bf = (kb * eg).astype(BF16)
    qg_bf = (q * eg).astype(BF16)
    same = bdb_ref[BDB_SAME]
    chunks = [_chunk_rows(c) for c in range(N_CHUNKS)]
    neg = cat_ref[CAT_NEG]
    strict = cat_ref[CAT_STRICT]
    rforms = _row_form([gc[rs] for rs in chunks], bdb_ref, pick_ref)
    decays = [jnp.exp(gc[rs] - rf + neg) for rs, rf in zip(chunks, rforms)]
    k_rows = [_rows(k_bf[rs], same) for rs in chunks]
    a_cats = [(_dg(kb_bf[rs], kr, 1, 1) * dc * strict).astype(BF16)
              for rs, kr, dc in zip(chunks, k_rows, decays)]
    qks = [(_dg(q_bf[rs], kr, 1, 1) * dc).astype(BF16) for rs, kr, dc in zip(chunks, k_rows, decays)]
    t_bds = _inv_unit_lower(a_cats, bdb_ref)
    uws = [_collapse(_dg(t_bd, jnp.concatenate([_rows(vb_bf[rs], same), _rows(kbg_bf[rs], same)], axis=1),
                         1, 0)) for rs, t_bd in zip(chunks, t_bds)]
    lasts = [gc[rs][CHUNK - 1:CHUNK, :] for rs in chunks]
    k_decs = [(k[rs] * jnp.exp(last - gc[rs])).astype(BF16) for rs, last in zip(chunks, lasts)]
    st = st_ref[...]
    same_f = bdf_ref[BDF_SAME]
    os_ = []
    for c, rs in enumerate(chunks):
        st_bf = st.astype(BF16)
        v_new = uws[c][:, 0:GROUP_W] - _dg(uws[c][:, GROUP_W:].astype(BF16), st_bf, 1, 0)
        v_new_bf = v_new.astype(BF16)
        os_.append(_dg(qg_bf[rs], st_bf, 1, 0) + _dg(qks[c], _rows(v_new_bf, same), 1, 0))
        st = st * jnp.exp(lasts[c]) + same_f * _dg(k_decs[c], v_new_bf, 0, 0)
    st_ref[...] = st
    o = jnp.concatenate(os_, axis=0)
    ms = _mm_sel_r2(o * o, ones_h) * (1.0 / HEAD_DIM)
    o_ref[0] = o * lax.rsqrt(ms + NORM_EPS) * nw_ref[...] * _silu(z)


RWKV_W = 896


def _rwkv_kernel(x_ref, halo_ref, bdf_ref, bdb_ref, cat_ref, sel_ref, exp_ref, pick_ref,
                 mu_ref, w0_ref, wup_ref, a0_ref, aup_ref, gup_ref, kk_ref, ka_ref,
                 rk_ref, lnw_ref, lnb_ref, o_ref, st_ref):
    _reset_state(st_ref)
    first = pl.program_id(1) == 0
    ones_h = sel_ref[SEL_ONES_H]
    x = x_ref[0]
    halo = jnp.where(first, 0.0, halo_ref[0])
    shifted = pltpu.roll(jnp.concatenate([halo, x], axis=0), 1, axis=0)[HALO:]
    x = x + mu_ref[...] * (shifted - x)
    r = x[:, 0:GROUP_W]
    k = x[:, GROUP_W:2 * GROUP_W]
    v = x[:, 2 * GROUP_W:3 * GROUP_W]
    lora = x[:, 3 * GROUP_W:]
    w_log = -_softplus(-(w0_ref[...] + _mm_sel_r2(jnp.tanh(lora), wup_ref[...]))) - RWKV_DECAY_OFFSET
    lw = -jnp.exp(w_log)
    a_gate = _sigmoid(a0_ref[...] + _mm_sel_r2(lora, aup_ref[...]))
    g = _mm_sel_r2(_sigmoid(lora), gup_ref[...])
    kk = k * kk_ref[...]
    kk = kk * lax.rsqrt(_mm_sel_r2(kk * kk, ones_h) + L2_EPS)
    k = k * (1.0 + (a_gate - 1.0) * ka_ref[...])
    b_v = kk * a_gate
    cl = _mm_sel_l(sel_ref[SEL_LTRI], lw)
    e_in = jnp.exp(cl)
    e_inv = jnp.exp(-cl)
    at_bf = (-kk * jnp.exp(cl - lw)).astype(BF16)
    rt_bf = (r * e_in).astype(BF16)
    bt_bf = (b_v * e_inv).astype(BF16)
    kt_bf = (k * e_inv).astype(BF16)
    v_bf = v.astype(BF16)
    same = bdb_ref[BDB_SAME]
    strict = cat_ref[CAT_STRICT]
    incl = cat_ref[CAT_INCL]
    chunks = [_chunk_rows(c) for c in range(N_CHUNKS)]
    zero = jnp.zeros((), F32)
    b_rows = [_rows(bt_bf[rs], same) for rs in chunks]
    k_rows = [_rows(kt_bf[rs], same) for rs in chunks]
    v_rows = [_rows(v_bf[rs], same) for rs in chunks]
    n_cats = [jnp.where(strict > 0, -_dg(at_bf[rs], br, 1, 1), zero).astype(BF16)
              for rs, br in zip(chunks, b_rows)]
    aks = [jnp.where(strict > 0, _dg(at_bf[rs], kr, 1, 1), zero).astype(BF16)
           for rs, kr in zip(chunks, k_rows)]
    rbs = [jnp.where(incl > 0, _dg(rt_bf[rs], br, 1, 1), zero).astype(BF16)
           for rs, br in zip(chunks, b_rows)]
    rks = [jnp.where(incl > 0, _dg(rt_bf[rs], kr, 1, 1), zero).astype(BF16)
           for rs, kr in zip(chunks, k_rows)]
    t_bds = _inv_unit_lower(n_cats, bdb_ref)
    tas = [_dg(t_bd, _rows(at_bf[rs], same), 1, 0).astype(BF16) for rs, t_bd in zip(chunks, t_bds)]
    tvs = [_dg(t_bd, _rows(_dg(ak, vr, 1, 0).astype(BF16), same), 1, 0)
           for t_bd, ak, vr in zip(t_bds, aks, v_rows)]
    y_vs = [_dg(rk, vr, 1, 0) for rk, vr in zip(rks, v_rows)]
    lasts = [cl[rs][CHUNK - 1:CHUNK, :] for rs in chunks]
    to_ends = [jnp.exp(last - cl[rs]) for rs, last in zip(chunks, lasts)]
    b_ends = [(b_v[rs] * te).astype(BF16) for rs, te in zip(chunks, to_ends)]
    upd_vs = [_dg(v_bf[rs], (k[rs] * te).astype(BF16), 0, 0) for rs, te in zip(chunks, to_ends)]
    st = st_ref[...]
    same_f = bdf_ref[BDF_SAME]
    ys = []
    for c, rs in enumerate(chunks):
        st_bf = st.astype(BF16)
        u_rows = (_dg(tas[c], st_bf, 1, 1) + tvs[c]).astype(BF16)
        ys.append(_dg(rt_bf[rs], st_bf, 1, 1) + _dg(rbs[c], u_rows, 1, 0) + y_vs[c])
        upd = _dg(_collapse(u_rows), b_ends[c], 0, 0) + upd_vs[c]
        st = st * jnp.exp(lasts[c]) + same_f * upd
    st_ref[...] = st
    y = jnp.concatenate(ys, axis=0)
    inv_d = 1.0 / HEAD_DIM
    mean = _mm_sel_r2(y, ones_h) * inv_d
    yc = y - mean
    var = _mm_sel_r2(yc * yc, ones_h) * inv_d
    yn = yc * lax.rsqrt(var + RWKV_GN_EPS) * lnw_ref[...] + lnb_ref[...]
    bonus = _mm_sel_r2(r * k * rk_ref[...], ones_h) * v
    o_ref[0] = (yn + bonus) * g


SB_BLOCK = 256


def _sb_kernel(q_ref, k_ref, v_ref, o_ref, acc_ref, run_ref):
    qi = pl.program_id(1)
    t = SB_BLOCK
    heads = range(N_HEADS)
    scale = HEAD_DIM ** -0.5
    assert scale == 0.125
    lane_head = _iota((t, GROUP_W), 1) // HEAD_DIM
    q = q_ref[0] * jnp.asarray(scale, q_ref.dtype)
    zero = jnp.zeros((), q.dtype)
    q_h = [jnp.where(lane_head == h, q, zero) for h in heads]
    upper = (_iota((t, t), 0) >= _iota((t, t), 1)).astype(BF16)
    causal = _iota((t, t), 1) < _iota((t, t), 0)
    acc_ref[...] = jnp.zeros_like(acc_ref)
    run_ref[...] = jnp.zeros_like(run_ref)

    def block(j, masked):
        start = pl.multiple_of(j * t, t)
        kj = k_ref[0, pl.ds(start, t), :]
        vj = v_ref[0, pl.ds(start, t), :]
        zz = [_dg(q_h[h], kj, 1, 1) for h in heads]
        parts, sums = [], []
        for h in heads:
            sp = jnp.maximum(zz[h], 0.0) + jnp.log(1.0 + jnp.exp(-jnp.abs(zz[h])))
            if masked:
                sp = jnp.where(causal, sp, 0.0)
            parts.append(_split2(sp))
            sums.append(jnp.sum(sp, axis=1, keepdims=True))
        later = [_dg(hi, upper, 1, 0) + _dg(lo, upper, 1, 0) for hi, lo in parts]
        wgts = []
        for h in heads:
            run = run_ref[h]
            wgt = jnp.exp(zz[h] - later[h] - pltpu.repeat(run, t // LANE, 1))
            if masked:
                wgt = jnp.where(causal, wgt, 0.0)
            wgts.append(wgt.astype(BF16))
            run_ref[h] = run + sums[h]
        v_cat = jnp.concatenate([jnp.where(lane_head == h, vj, zero) for h in heads], axis=0)
        acc_ref[...] += _dg(jnp.concatenate(wgts, axis=1), v_cat, 1, 0)

    block(qi, True)

    def body(jj, carry):
        block(qi - 1 - jj, False)
        return carry

    lax.fori_loop(0, qi, body, 0)
    o_ref[0] = acc_ref[...]


def _sb(pd):
    bsz, s, _ = pd.shape
    t = SB_BLOCK
    return pl.pallas_call(
        _sb_kernel,
        grid=(bsz, s // t),
        in_specs=[pl.BlockSpec((1, t, GROUP_W), lambda b, i: (b, i, 0)),
                  pl.BlockSpec((1, s, GROUP_W), lambda b, i: (b, 0, 1)),
                  pl.BlockSpec((1, s, GROUP_W), lambda b, i: (b, 0, 2))],
        out_specs=pl.BlockSpec((1, t, GROUP_W), lambda b, i: (b, i, 0)),
        out_shape=jax.ShapeDtypeStruct((bsz, s, GROUP_W), F32),
        scratch_shapes=[pltpu.VMEM((t, GROUP_W), F32), pltpu.VMEM((N_HEADS, t, LANE), F32)],
        compiler_params=_params(("parallel", "arbitrary")),
        name="stickbreak",
    )(pd, pd, pd)


def _row(v, width=None):
    v = v.reshape(1, -1).astype(F32)
    if width is not None and v.shape[1] < width:
        v = jnp.pad(v, ((0, 0), (0, width - v.shape[1])))
    return v


def _pad_cols(w, width):
    return jnp.pad(w, ((0, 0), (0, width - w.shape[1])))


def _pad_rows_at(w, offset, total):
    return jnp.pad(w, ((offset, total - offset - w.shape[0]), (0, 0)))


def _mixer(h2, bsz, s, gain, w_in, ssm_conv_w, ssm_conv_b, ssm_dt_bias, ssm_a_log, ssm_d, ssm_norm,
           rwkv_mu, rwkv_w0, rwkv_w_up, rwkv_a0, rwkv_a_up, rwkv_g_up, rwkv_k_k, rwkv_k_a, rwkv_r_k,
           rwkv_ln_w, rwkv_ln_b, gdn_conv_w, gdn_a_log, gdn_dt_bias, gdn_norm, w_out, tm):
    gw = GROUP_W
    ssm_in = 4 * gw + N_HEADS
    c0 = ssm_in
    c1 = c0 + RWKV_W
    c2 = c1 + 4 * gw + 2 * N_HEADS
    wa = _pad_cols(w_in[:, :c0], SSD_W).astype(BF16)
    wb = w_in[:, c0:c1].astype(BF16)
    wc = _pad_cols(w_in[:, c1:c2], GDN_W).astype(BF16)
    wd = w_in[:, c2:].astype(BF16)
    pa, pb, pc, pd = _inproj(h2, _row(gain), [wa, wb, wc, wd], tm)

    ya = _mixer_call(_ssd_kernel, "ssd", pa.reshape(bsz, s, SSD_W),
                     [ssm_conv_w.astype(F32), _row(ssm_conv_b), _row(ssm_dt_bias, LANE),
                      _row(ssm_a_log, LANE), _row(jnp.repeat(ssm_d, HEAD_DIM)), _row(ssm_norm)])

    lora_w = RWKV_W - 3 * gw
    yb = _mixer_call(_rwkv_kernel, "rwkv", pb.reshape(bsz, s, RWKV_W),
                     [_row(rwkv_mu), _row(rwkv_w0),
                      _pad_rows_at(rwkv_w_up, 0, lora_w).astype(BF16), _row(rwkv_a0),
                      _pad_rows_at(rwkv_a_up, rwkv_w_up.shape[0], lora_w).astype(BF16),
                      _pad_rows_at(rwkv_g_up, rwkv_w_up.shape[0] + rwkv_a_up.shape[0], lora_w).astype(BF16),
                      _row(rwkv_k_k), _row(rwkv_k_a), _row(rwkv_r_k), _row(rwkv_ln_w), _row(rwkv_ln_b)])

    pad4 = lambda t: jnp.pad(t.reshape(1, -1).astype(F32), ((0, 0), (N_HEADS, LANE - 2 * N_HEADS)))
    yc = _mixer_call(_gdn_kernel, "gdn", pc.reshape(bsz, s, GDN_W),
                     [gdn_conv_w.astype(F32), pad4(gdn_a_log), pad4(gdn_dt_bias),
                      _row(jnp.tile(gdn_norm, N_HEADS))])

    yd = _sb(pd.reshape(bsz, s, 3 * gw))

    n = bsz * s
    return _outproj(h2, [ya.reshape(n, gw), yb.reshape(n, gw), yc.reshape(n, gw), yd.reshape(n, gw)],
                    w_out.astype(BF16), tm)


def kernel(x, p, ffn1_norm, ffn1_w_gate, ffn1_w_up, ffn1_w_down, mix_norm, w_in, ssm_conv_w, ssm_conv_b, ssm_dt_bias, ssm_a_log, ssm_d, ssm_norm, rwkv_mu, rwkv_w0, rwkv_w_up, rwkv_a0, rwkv_a_up, rwkv_g_up, rwkv_k_k, rwkv_k_a, rwkv_r_k, rwkv_ln_w, rwkv_ln_b, gdn_conv_w, gdn_a_log, gdn_dt_bias, gdn_norm, w_out, ffn2_norm, ffn2_w_gate, ffn2_w_up, ffn2_w_down, ple_norm, ple_w_gate, ple_w_proj, final_norm):
    bsz, s, d = x.shape
    depth = p.shape[0]
    n = bsz * s
    tm = min(512, n)
    h = x.reshape(n, d)
    for i in range(depth):
        h = _ffn(h, _row(ffn1_norm[i]), ffn1_w_gate[i].astype(BF16), ffn1_w_up[i].astype(BF16),
                 ffn1_w_down[i].astype(BF16), tm)
        h = _mixer(h, bsz, s, mix_norm[i], w_in[i], ssm_conv_w[i], ssm_conv_b[i], ssm_dt_bias[i],
                   ssm_a_log[i], ssm_d[i], ssm_norm[i], rwkv_mu[i], rwkv_w0[i], rwkv_w_up[i],
                   rwkv_a0[i], rwkv_a_up[i], rwkv_g_up[i], rwkv_k_k[i], rwkv_k_a[i], rwkv_r_k[i],
                   rwkv_ln_w[i], rwkv_ln_b[i], gdn_conv_w[i], gdn_a_log[i], gdn_dt_bias[i],
                   gdn_norm[i], w_out[i], tm)
        h = _ffn(h, _row(ffn2_norm[i]), ffn2_w_gate[i].astype(BF16), ffn2_w_up[i].astype(BF16),
                 ffn2_w_down[i].astype(BF16), tm)
        h = _ple(h, p[i].reshape(n, -1), _row(ple_norm[i]), ple_w_gate[i].astype(BF16),
                 ple_w_proj[i].astype(BF16), _row(final_norm), i == depth - 1, tm)
    return h.reshape(bsz, s, d)
```

```python
import functools

import numpy as np
import jax
import jax.numpy as jnp
from jax import lax
from jax.experimental import pallas as pl
from jax.experimental.pallas import tpu as pltpu

F32 = jnp.float32
BF16 = jnp.bfloat16

HEAD_DIM = 64
N_HEADS = 4
GROUP_W = HEAD_DIM * N_HEADS
CHUNK = 64
N_CHUNKS = 4
CONV_K = 4
SSM_STATE = 128
NORM_EPS = 1e-6
L2_EPS = 1e-6
RWKV_GN_EPS = 64e-5
RWKV_DECAY_OFFSET = 0.5
HALO = 8
LANE = 128
VMEM_LIMIT = 56 * 1024 * 1024


def _dg(a, b, ca, cb):
    return lax.dot_general(a, b, (((ca,), (cb,)), ((), ())), preferred_element_type=F32)


_DIMS = {"nn": (1, 0), "nt": (1, 1), "tn": (0, 0)}


def _mm(a, b, kind="nn"):
    ca, cb = _DIMS[kind]
    return _dg(a.astype(BF16), b.astype(BF16), ca, cb)


def _split2(x):
    hi = x.astype(BF16)
    lo = (x - hi.astype(F32)).astype(BF16)
    return hi, lo


def _split3(x):
    hi = x.astype(BF16)
    r = x - hi.astype(F32)
    mid = r.astype(BF16)
    lo = (r - mid.astype(F32)).astype(BF16)
    return hi, mid, lo


def _mm_sel_l(sel, b, kind="nn"):
    ca, cb = _DIMS[kind]
    h, m, l = _split3(b)
    return _dg(sel, h, ca, cb) + (_dg(sel, m, ca, cb) + _dg(sel, l, ca, cb))


def _mm_sel_r(a, sel, kind="nn"):
    ca, cb = _DIMS[kind]
    h, m, l = _split3(a)
    return _dg(h, sel, ca, cb) + (_dg(m, sel, ca, cb) + _dg(l, sel, ca, cb))


def _mm_sel_r2(a, sel):
    h, l = _split2(a)
    return _dg(h, sel, 1, 0) + _dg(l, sel, 1, 0)


def _sigmoid(x):
    return 1.0 / (1.0 + jnp.exp(-x))


def _silu(x):
    return x * _sigmoid(x)


def _softplus(x):
    return jnp.maximum(x, 0.0) + jnp.log1p(jnp.exp(-jnp.abs(x)))


def _iota(shape, dim):
    return lax.broadcasted_iota(jnp.int32, shape, dim)


def _tile_rows(x, n):
    return jnp.concatenate([x] * n, axis=0)


def _rms(x, gain_row):
    ms = jnp.mean(x * x, axis=-1, keepdims=True)
    return x * lax.rsqrt(ms + NORM_EPS) * gain_row


BDF_SAME, BDF_GRP_STATE = range(2)
BDB_SAME, BDB_EYE, BDB_M8, BDB_OFF8, BDB_OFF16, BDB_OFF32, BDB_GRP_ROWS = range(7)
CAT_NEG, CAT_STRICT, CAT_INCL = range(3)
SEL_ONES_H, SEL_ONES_G, SEL_LTRI = range(3)


def _np_consts():
    r = np.arange(GROUP_W)[:, None]
    c = np.arange(GROUP_W)[None, :]
    same = (r // HEAD_DIM) == (c // HEAD_DIM)

    def off(s):
        return ((r // (2 * s)) == (c // (2 * s))) & (((r // s) % 2) == 1) & (((c // s) % 2) == 0)

    bdf = np.stack([same, (r // SSM_STATE) == ((c // HEAD_DIM) // 2)]).astype(np.float32)
    bdb = np.stack([same, r == c, (r // 8) == (c // 8), off(8), off(16), off(32),
                    ((r // HEAD_DIM) // 2) == (c // SSM_STATE)]).astype(np.float32)
    l = np.arange(CHUNK)[:, None]
    s = np.arange(GROUP_W)[None, :] % CHUNK
    cat = np.stack([np.where(l >= s, 0.0, -np.inf), l > s, l >= s]).astype(np.float32)
    t = np.arange(N_CHUNKS * CHUNK)
    ltri = (t[:, None] >= t[None, :]) & ((t[:, None] // CHUNK) == (t[None, :] // CHUNK))
    sel = np.stack([same, (r // SSM_STATE) == (c // SSM_STATE), ltri]).astype(np.float32)
    er = np.arange(LANE)[:, None]
    ec = np.arange(GROUP_W)[None, :] // HEAD_DIM
    expand = np.stack([er == ec, er == ec + N_HEADS]).astype(np.float32)
    pick = np.broadcast_to(np.arange(GROUP_W)[None, :] % HEAD_DIM == 0, (CHUNK, GROUP_W)).astype(np.float32)
    return (jnp.asarray(bdf), jnp.asarray(bdb, BF16), jnp.asarray(cat), jnp.asarray(sel, BF16),
            jnp.asarray(expand, BF16), jnp.asarray(pick, BF16))


def _rows(x_bf, mask_bf):
    return _tile_rows(x_bf, N_HEADS) * mask_bf


def _row_form(colvals, bdb_ref, pick_ref):
    same = bdb_ref[BDB_SAME]
    pick = pick_ref[...]
    parts = [_split3(x) for x in colvals]
    prods = [[_dg(pick, _rows(p, same), 1, 1) for p in ps] for ps in parts]
    return [h + (m + l) for h, m, l in prods]


def _collapse(rows):
    out = rows[0:CHUNK]
    for h in range(1, N_HEADS):
        out = out + rows[h * CHUNK:(h + 1) * CHUNK]
    return out


def _inv_unit_lower(n_cats_bf, bdb_ref):
    tiled = [_tile_rows(n, N_HEADS) for n in n_cats_bf]
    eye = bdb_ref[BDB_EYE]
    m8 = bdb_ref[BDB_M8]
    d = [x * m8 for x in tiled]
    d2 = [_dg(x, x, 1, 0).astype(BF16) for x in d]
    d4 = [_dg(x, x, 1, 0).astype(BF16) for x in d2]
    t = [_dg(eye - x, eye + y, 1, 0) for x, y in zip(d, d2)]
    t = [_dg(x.astype(BF16), eye + y, 1, 0) for x, y in zip(t, d4)]
    for idx in (BDB_OFF8, BDB_OFF16, BDB_OFF32):
        off = bdb_ref[idx]
        tb = [x.astype(BF16) for x in t]
        to = [_dg(x, y * off, 1, 0).astype(BF16) for x, y in zip(tb, tiled)]
        t = [x - _dg(y, z, 1, 0) for x, y, z in zip(t, to, tb)]
    return [x.astype(BF16) for x in t]


def _causal_conv(cur, halo, w_ref, first):
    halo = jnp.where(first, 0.0, halo)
    ext = jnp.concatenate([halo, cur], axis=0)
    acc = cur * w_ref[CONV_K - 1:CONV_K, :]
    for j in range(CONV_K - 1):
        sh = pltpu.roll(ext, CONV_K - 1 - j, axis=0)[HALO:]
        acc = acc + sh * w_ref[j:j + 1, :]
    return acc


def _chunk_rows(c):
    return slice(c * CHUNK, (c + 1) * CHUNK)


def _ffn_kernel(h_ref, g_ref, wg_ref, wu_ref, wd_ref, o_ref):
    x = h_ref[...]
    u = _rms(x, g_ref[...]).astype(BF16)
    a = jnp.dot(u, wg_ref[...], preferred_element_type=F32)
    b = jnp.dot(u, wu_ref[...], preferred_element_type=F32)
    act = (_silu(a) * b).astype(BF16)
    y = jnp.dot(act, wd_ref[...], preferred_element_type=F32)
    o_ref[...] = x + 0.5 * y


def _const_spec(shape):
    nd = len(shape)
    return pl.BlockSpec(shape, lambda *_: (0,) * nd)


def _params(sem):
    return pltpu.CompilerParams(dimension_semantics=sem, vmem_limit_bytes=VMEM_LIMIT)


def _ffn(h, gain, wg, wu, wd, tm):
    n, d = h.shape
    f = wg.shape[1]
    return pl.pallas_call(
        _ffn_kernel,
        grid=(n // tm,),
        in_specs=[pl.BlockSpec((tm, d), lambda i: (i, 0)), _const_spec((1, d)),
                  _const_spec((d, f)), _const_spec((d, f)), _const_spec((f, d))],
        out_specs=pl.BlockSpec((tm, d), lambda i: (i, 0)),
        out_shape=jax.ShapeDtypeStruct((n, d), F32),
        compiler_params=_params(("parallel",)),
        name="ffn",
    )(h, gain, wg, wu, wd)


def _inproj_kernel(h_ref, g_ref, wa_ref, wb_ref, wc_ref, wd_ref, oa_ref, ob_ref, oc_ref, od_ref):
    u = _rms(h_ref[...], g_ref[...]).astype(BF16)
    for w_ref, o_ref in ((wa_ref, oa_ref), (wb_ref, ob_ref), (wc_ref, oc_ref), (wd_ref, od_ref)):
        o_ref[...] = jnp.dot(u, w_ref[...], preferred_element_type=F32).astype(o_ref.dtype)


def _inproj(h, gain, ws, tm):
    n, d = h.shape
    widths = [w.shape[1] for w in ws]
    dtypes = [F32, F32, F32, BF16]
    return pl.pallas_call(
        _inproj_kernel,
        grid=(n // tm,),
        in_specs=[pl.BlockSpec((tm, d), lambda i: (i, 0)), _const_spec((1, d))]
                 + [_const_spec((d, wd)) for wd in widths],
        out_specs=[pl.BlockSpec((tm, wd), lambda i: (i, 0)) for wd in widths],
        out_shape=[jax.ShapeDtypeStruct((n, wd), dt) for wd, dt in zip(widths, dtypes)],
        compiler_params=_params(("parallel",)),
        name="inproj",
    )(h, gain, *ws)


def _outproj_kernel(h_ref, ya_ref, yb_ref, yc_ref, yd_ref, w_ref, o_ref):
    acc = h_ref[...]
    for i, y_ref in enumerate((ya_ref, yb_ref, yc_ref, yd_ref)):
        acc = acc + jnp.dot(y_ref[...].astype(BF16), w_ref[i * GROUP_W:(i + 1) * GROUP_W, :],
                            preferred_element_type=F32)
    o_ref[...] = acc


def _outproj(h, ys, w, tm):
    n, d = h.shape
    return pl.pallas_call(
        _outproj_kernel,
        grid=(n // tm,),
        in_specs=[pl.BlockSpec((tm, d), lambda i: (i, 0))]
                 + [pl.BlockSpec((tm, GROUP_W), lambda i: (i, 0)) for _ in ys]
                 + [_const_spec(w.shape)],
        out_specs=pl.BlockSpec((tm, d), lambda i: (i, 0)),
        out_shape=jax.ShapeDtypeStruct((n, d), F32),
        compiler_params=_params(("parallel",)),
        name="outproj",
    )(h, *ys, w)


def _ple_kernel(h_ref, p_ref, g_ref, wg_ref, wp_ref, fg_ref, o_ref, *, final):
    x = h_ref[...]
    u = _rms(x, g_ref[...]).astype(BF16)
    gate = _sigmoid(jnp.dot(u, wg_ref[...], preferred_element_type=F32))
    e = jnp.dot(p_ref[...].astype(BF16), wp_ref[...], preferred_element_type=F32)
    y = x + e * gate
    if final:
        y = _rms(y, fg_ref[...])
    o_ref[...] = y


def _ple(h, p, gain, wg, wp, fgain, final, tm):
    n, d = h.shape
    pd = p.shape[1]
    return pl.pallas_call(
        functools.partial(_ple_kernel, final=final),
        grid=(n // tm,),
        in_specs=[pl.BlockSpec((tm, d), lambda i: (i, 0)), pl.BlockSpec((tm, pd), lambda i: (i, 0)),
                  _const_spec((1, d)), _const_spec((d, d)), _const_spec((pd, d)), _const_spec((1, d))],
        out_specs=pl.BlockSpec((tm, d), lambda i: (i, 0)),
        out_shape=jax.ShapeDtypeStruct((n, d), F32),
        compiler_params=_params(("parallel",)),
        name="ple",
    )(h, p, gain, wg, wp, fgain)


def _halo_map(rows_per_block):
    step = rows_per_block // HALO
    return lambda b, i: (b, jnp.maximum(i * step - 1, 0), 0)


def _mixer_call(kernel_fn, name, x, params):
    bsz, s, w = x.shape
    t = N_CHUNKS * CHUNK
    consts = _np_consts()
    return pl.pallas_call(
        kernel_fn,
        grid=(bsz, s // t),
        in_specs=[pl.BlockSpec((1, t, w), lambda b, i: (b, i, 0)),
                  pl.BlockSpec((1, HALO, w), _halo_map(t))]
                 + [_const_spec(c.shape) for c in consts]
                 + [_const_spec(p.shape) for p in params],
        out_specs=pl.BlockSpec((1, t, GROUP_W), lambda b, i: (b, i, 0)),
        out_shape=jax.ShapeDtypeStruct((bsz, s, GROUP_W), F32),
        scratch_shapes=[pltpu.VMEM((GROUP_W, GROUP_W), F32)],
        compiler_params=_params(("parallel", "arbitrary")),
        name=name,
    )(x, x, *consts, *params)


def _reset_state(st_ref):
    @pl.when(pl.program_id(1) == 0)
    def _():
        st_ref[...] = jnp.zeros_like(st_ref)


SSD_W = 1152


def _ssd_kernel(x_ref, halo_ref, bdf_ref, bdb_ref, cat_ref, sel_ref, exp_ref, pick_ref,
                cw_ref, cb_ref, dtb_ref, alog_ref, dsk_ref, nw_ref, o_ref, st_ref):
    _reset_state(st_ref)
    first = pl.program_id(1) == 0
    x = x_ref[0]
    z = x[:, 0:GROUP_W]
    xbc = _causal_conv(x[:, GROUP_W:4 * GROUP_W], halo_ref[0][:, GROUP_W:4 * GROUP_W], cw_ref, first)
    xbc = _silu(xbc + cb_ref[...])
    xs = xbc[:, 0:GROUP_W]
    bm = xbc[:, GROUP_W:2 * GROUP_W].astype(BF16)
    cm = xbc[:, 2 * GROUP_W:3 * GROUP_W].astype(BF16)
    dt_pad = _softplus(x[:, 4 * GROUP_W:] + dtb_ref[...])
    la_pad = dt_pad * (-jnp.exp(alog_ref[...]))
    dt_b = _mm_sel_r(dt_pad, exp_ref[0])
    la_b = _mm_sel_r(la_pad, exp_ref[0])
    cs = _mm_sel_l(sel_ref[SEL_LTRI], la_b)
    xc = xs * dt_b
    ecs = jnp.exp(cs)
    same = bdb_ref[BDB_SAME]
    grp_rows = bdb_ref[BDB_GRP_ROWS]
    grp_state = bdf_ref[BDF_GRP_STATE]
    chunks = [_chunk_rows(c) for c in range(N_CHUNKS)]
    neg = cat_ref[CAT_NEG]
    rforms = _row_form([cs[rs] for rs in chunks], bdb_ref, pick_ref)
    segs = [jnp.exp(cs[rs] - rf + neg) for rs, rf in zip(chunks, rforms)]
    scores = [(_dg(cm[rs], _rows(bm[rs], grp_rows), 1, 1) * sg).astype(BF16)
              for rs, sg in zip(chunks, segs)]
    xc_bf = xc.astype(BF16)
    ys = [_dg(sc, _rows(xc_bf[rs], same), 1, 0) for rs, sc in zip(chunks, scores)]
    lasts = [cs[rs][CHUNK - 1:CHUNK, :] for rs in chunks]
    upds = [_dg(bm[rs], (xc[rs] * jnp.exp(last - cs[rs])).astype(BF16), 0, 0) * grp_state
            for rs, last in zip(chunks, lasts)]
    st = st_ref[...]
    for c, rs in enumerate(chunks):
        ys[c] = ys[c] + _dg(cm[rs], st.astype(BF16), 1, 0) * ecs[rs]
        st = st * jnp.exp(lasts[c]) + upds[c]
    st_ref[...] = st
    y = (jnp.concatenate(ys, axis=0) + dsk_ref[...] * xs) * _silu(z)
    ms = _mm_sel_r2(y * y, sel_ref[SEL_ONES_G]) * (1.0 / SSM_STATE)
    o_ref[0] = y * lax.rsqrt(ms + NORM_EPS) * nw_ref[...]


GDN_W = 1152


def _gdn_kernel(x_ref, halo_ref, bdf_ref, bdb_ref, cat_ref, sel_ref, exp_ref, pick_ref,
                cw_ref, alog_ref, dtb_ref, nw_ref, o_ref, st_ref):
    _reset_state(st_ref)
    first = pl.program_id(1) == 0
    ones_h = sel_ref[SEL_ONES_H]
    x = x_ref[0]
    qkv = _silu(_causal_conv(x[:, 0:3 * GROUP_W], halo_ref[0][:, 0:3 * GROUP_W], cw_ref, first))
    z = x[:, 3 * GROUP_W:4 * GROUP_W]
    ba = x[:, 4 * GROUP_W:]
    q = qkv[:, 0:GROUP_W]
    k = qkv[:, GROUP_W:2 * GROUP_W]
    v = qkv[:, 2 * GROUP_W:3 * GROUP_W]
    q = q * lax.rsqrt(_mm_sel_r2(q * q, ones_h) + L2_EPS) * (HEAD_DIM ** -0.5)
    k = k * lax.rsqrt(_mm_sel_r2(k * k, ones_h) + L2_EPS)
    beta_pad = _sigmoid(ba)
    g_pad = -jnp.exp(alog_ref[...]) * _softplus(ba + dtb_ref[...])
    beta_b = _mm_sel_r(beta_pad, exp_ref[0])
    g_b = _mm_sel_r(g_pad, exp_ref[1])
    gc = _mm_sel_l(sel_ref[SEL_LTRI], g_b)
    eg = jnp.exp(gc)
    kb = k * beta_b
    q_bf = q.astype(BF16)
    k_bf = k.astype(BF16)
    kb_bf = kb.astype(BF16)
    vb_bf = (v * beta_b).astype(BF16)
    kbg_bf = (kb * eg).astype(BF16)
    qg_bf = (q * eg).astype(BF16)
    same = bdb_ref[BDB_SAME]
    chunks = [_chunk_rows(c) for c in range(N_CHUNKS)]
    neg = cat_ref[CAT_NEG]
    strict = cat_ref[CAT_STRICT]
    rforms = _row_form([gc[rs] for rs in chunks], bdb_ref, pick_ref)
    decays = [jnp.exp(gc[rs] - rf + neg) for rs, rf in zip(chunks, rforms)]
    k_rows = [_rows(k_bf[rs], same) for rs in chunks]
    a_cats = [(_dg(kb_bf[rs], kr, 1, 1) * dc * strict).astype(BF16)
              for rs, kr, dc in zip(chunks, k_rows, decays)]
    qks = [(_dg(q_bf[rs], kr, 1, 1) * dc).astype(BF16) for rs, kr, dc in zip(chunks, k_rows, decays)]
    t_bds = _inv_unit_lower(a_cats, bdb_ref)
    uws = [_collapse(_dg(t_bd, jnp.concatenate([_rows(vb_bf[rs], same), _rows(kbg_bf[rs], same)], axis=1),
                         1, 0)) for rs, t_bd in zip(chunks, t_bds)]
    lasts = [gc[rs][CHUNK - 1:CHUNK, :] for rs in chunks]
    k_decs = [(k[rs] * jnp.exp(last - gc[rs])).astype(BF16) for rs, last in zip(chunks, lasts)]
    same_f = bdf_ref[BDF_SAME]
    us = [uw[:, 0:GROUP_W] for uw in uws]
    ws_bf = [uw[:, GROUP_W:].astype(BF16) for uw in uws]
    m_st = [(-same_f * _dg(kd, w, 0, 0)).astype(BF16) for kd, w in zip(k_decs, ws_bf)]
    c_st = [same_f * _dg(kd, u.astype(BF16), 0, 0) for kd, u in zip(k_decs, us)]
    st = st_ref[...]
    sts = []
    for c in range(N_CHUNKS):
        st_bf = st.astype(BF16)
        sts.append(st_bf)
        st = st * jnp.exp(lasts[c]) + _dg(m_st[c], st_bf, 1, 0) + c_st[c]
    st_ref[...] = st
    v_news = [(u - _dg(w, s_bf, 1, 0)).astype(BF16) for u, w, s_bf in zip(us, ws_bf, sts)]
    os_ = [_dg(qg_bf[rs], s_bf, 1, 0) + _dg(qk, _rows(vn, same), 1, 0)
           for rs, s_bf, qk, vn in zip(chunks, sts, qks, v_news)]
    o = jnp.concatenate(os_, axis=0)
    ms = _mm_sel_r2(o * o, ones_h) * (1.0 / HEAD_DIM)
    o_ref[0] = o * lax.rsqrt(ms + NORM_EPS) * nw_ref[...] * _silu(z)


RWKV_W = 896


def _rwkv_kernel(x_ref, halo_ref, bdf_ref, bdb_ref, cat_ref, sel_ref, exp_ref, pick_ref,
                 mu_ref, w0_ref, wup_ref, a0_ref, aup_ref, gup_ref, kk_ref, ka_ref,
                 rk_ref, lnw_ref, lnb_ref, o_ref, st_ref):
    _reset_state(st_ref)
    first = pl.program_id(1) == 0
    ones_h = sel_ref[SEL_ONES_H]
    x = x_ref[0]
    halo = jnp.where(first, 0.0, halo_ref[0])
    shifted = pltpu.roll(jnp.concatenate([halo, x], axis=0), 1, axis=0)[HALO:]
    x = x + mu_ref[...] * (shifted - x)
    r = x[:, 0:GROUP_W]
    k = x[:, GROUP_W:2 * GROUP_W]
    v = x[:, 2 * GROUP_W:3 * GROUP_W]
    lora = x[:, 3 * GROUP_W:]
    w_log = -_softplus(-(w0_ref[...] + _mm_sel_r2(jnp.tanh(lora), wup_ref[...]))) - RWKV_DECAY_OFFSET
    lw = -jnp.exp(w_log)
    a_gate = _sigmoid(a0_ref[...] + _mm_sel_r2(lora, aup_ref[...]))
    g = _mm_sel_r2(_sigmoid(lora), gup_ref[...])
    kk = k * kk_ref[...]
    kk = kk * lax.rsqrt(_mm_sel_r2(kk * kk, ones_h) + L2_EPS)
    k = k * (1.0 + (a_gate - 1.0) * ka_ref[...])
    b_v = kk * a_gate
    cl = _mm_sel_l(sel_ref[SEL_LTRI], lw)
    e_in = jnp.exp(cl)
    e_inv = jnp.exp(-cl)
    at_bf = (-kk * jnp.exp(cl - lw)).astype(BF16)
    rt_bf = (r * e_in).astype(BF16)
    bt_bf = (b_v * e_inv).astype(BF16)
    kt_bf = (k * e_inv).astype(BF16)
    v_bf = v.astype(BF16)
    same = bdb_ref[BDB_SAME]
    strict = cat_ref[CAT_STRICT]
    incl = cat_ref[CAT_INCL]
    chunks = [_chunk_rows(c) for c in range(N_CHUNKS)]
    zero = jnp.zeros((), F32)
    b_rows = [_rows(bt_bf[rs], same) for rs in chunks]
    k_rows = [_rows(kt_bf[rs], same) for rs in chunks]
    v_rows = [_rows(v_bf[rs], same) for rs in chunks]
    n_cats = [jnp.where(strict > 0, -_dg(at_bf[rs], br, 1, 1), zero).astype(BF16)
              for rs, br in zip(chunks, b_rows)]
    aks = [jnp.where(strict > 0, _dg(at_bf[rs], kr, 1, 1), zero).astype(BF16)
           for rs, kr in zip(chunks, k_rows)]
    rbs = [jnp.where(incl > 0, _dg(rt_bf[rs], br, 1, 1), zero).astype(BF16)
           for rs, br in zip(chunks, b_rows)]
    rks = [jnp.where(incl > 0, _dg(rt_bf[rs], kr, 1, 1), zero).astype(BF16)
           for rs, kr in zip(chunks, k_rows)]
    t_bds = _inv_unit_lower(n_cats, bdb_ref)
    tas = [_dg(t_bd, _rows(at_bf[rs], same), 1, 0).astype(BF16) for rs, t_bd in zip(chunks, t_bds)]
    tvs = [_dg(t_bd, _rows(_dg(ak, vr, 1, 0).astype(BF16), same), 1, 0)
           for t_bd, ak, vr in zip(t_bds, aks, v_rows)]
    y_vs = [_dg(rk, vr, 1, 0) for rk, vr in zip(rks, v_rows)]
    lasts = [cl[rs][CHUNK - 1:CHUNK, :] for rs in chunks]
    to_ends = [jnp.exp(last - cl[rs]) for rs, last in zip(chunks, lasts)]
    same_f = bdf_ref[BDF_SAME]
    b_end_rows = [_rows((b_v[rs] * te).astype(BF16), same) for rs, te in zip(chunks, to_ends)]
    m_st = [_dg(ta, br, 0, 0).astype(BF16) for ta, br in zip(tas, b_end_rows)]
    c_st = [_dg(tv.astype(BF16), br, 0, 0) + same_f * _dg(v_bf[rs], (k[rs] * te).astype(BF16), 0, 0)
            for tv, br, rs, te in zip(tvs, b_end_rows, chunks, to_ends)]
    st = st_ref[...]
    sts = []
    for c in range(N_CHUNKS):
        st_bf = st.astype(BF16)
        sts.append(st_bf)
        st = st * jnp.exp(lasts[c]) + _dg(st_bf, m_st[c], 1, 0) + c_st[c]
    st_ref[...] = st
    u_rows = [(_dg(ta, s_bf, 1, 1) + tv).astype(BF16) for ta, tv, s_bf in zip(tas, tvs, sts)]
    ys = [_dg(rt_bf[rs], s_bf, 1, 1) + _dg(rb, ur, 1, 0) + yv
          for rs, s_bf, rb, ur, yv in zip(chunks, sts, rbs, u_rows, y_vs)]
    y = jnp.concatenate(ys, axis=0)
    inv_d = 1.0 / HEAD_DIM
    mean = _mm_sel_r2(y, ones_h) * inv_d
    yc = y - mean
    var = _mm_sel_r2(yc * yc, ones_h) * inv_d
    yn = yc * lax.rsqrt(var + RWKV_GN_EPS) * lnw_ref[...] + lnb_ref[...]
    bonus = _mm_sel_r2(r * k * rk_ref[...], ones_h) * v
    o_ref[0] = (yn + bonus) * g


SB_BLOCK = 256


def _neg_abs(x):
    sign = jnp.uint32(0x80000000)
    return lax.bitcast_convert_type(lax.bitcast_convert_type(x, jnp.uint32) | sign, F32)


def _sb_kernel(q_ref, k_ref, v_ref, o_ref, acc_ref, run_ref, zz_ref, w_ref):
    qi = pl.program_id(1)
    t = SB_BLOCK
    heads = range(N_HEADS)
    scale = HEAD_DIM ** -0.5
    assert scale == 0.125
    lane_head = _iota((t, GROUP_W), 1) // HEAD_DIM
    q = q_ref[0] * jnp.asarray(scale, q_ref.dtype)
    zero = jnp.zeros((), q.dtype)
    q_h = [jnp.where(lane_head == h, q, zero) for h in heads]
    tri = (_iota((t, t), 0) >= _iota((t, t), 1)).astype(BF16)
    upper2 = jnp.concatenate([tri, tri], axis=0)
    causal = _iota((t, t), 1) < _iota((t, t), 0)
    acc_ref[...] = jnp.zeros_like(acc_ref)
    run_ref[...] = jnp.zeros_like(run_ref)

    def scores(j):
        kj = k_ref[0, pl.ds(pl.multiple_of(j * t, t), t), :]
        return [_dg(q_h[h], kj, 1, 1) for h in heads]

    def weights(zz, masked):
        parts, sums = [], []
        for h in heads:
            sp = jnp.maximum(zz[h], 0.0) + jnp.log(1.0 + jnp.exp(_neg_abs(zz[h])))
            if masked:
                sp = jnp.where(causal, sp, 0.0)
            hi, lo = _split2(sp)
            parts.append(jnp.concatenate([hi, lo], axis=1))
            sums.append(jnp.sum(sp, axis=1, keepdims=True))
        later = [_dg(p, upper2, 1, 0) for p in parts]
        wgts = []
        for h in heads:
            run = run_ref[h]
            wgt = jnp.exp(zz[h] - later[h] - jnp.concatenate([run] * (t // LANE), axis=1))
            if masked:
                wgt = jnp.where(causal, wgt, 0.0)
            wgts.append(wgt.astype(BF16))
            run_ref[h] = run + sums[h]
        return jnp.concatenate(wgts, axis=1)

    def accumulate(w_cat, j):
        vj = v_ref[0, pl.ds(pl.multiple_of(j * t, t), t), :]
        v_cat = jnp.concatenate([jnp.where(lane_head == h, vj, zero) for h in heads], axis=0)
        acc_ref[...] += _dg(w_cat, v_cat, 1, 0)

    def stash(zz, w_cat):
        for h in heads:
            zz_ref[h] = zz[h]
        w_ref[...] = w_cat

    zz_next = scores(jnp.maximum(qi - 1, 0))
    stash(zz_next, weights(scores(qi), True))

    def body(i, carry):
        j = qi - i
        accumulate(w_ref[...], j + 1)
        zz_next = scores(jnp.maximum(j - 1, 0))
        stash(zz_next, weights([zz_ref[h] for h in heads], False))
        return carry

    lax.fori_loop(1, qi + 1, body, 0)
    accumulate(w_ref[...], 0)
    o_ref[0] = acc_ref[...]


def _sb(pd):
    bsz, s, _ = pd.shape
    t = SB_BLOCK
    return pl.pallas_call(
        _sb_kernel,
        grid=(bsz, s // t),
        in_specs=[pl.BlockSpec((1, t, GROUP_W), lambda b, i: (b, i, 0)),
                  pl.BlockSpec((1, s, GROUP_W), lambda b, i: (b, 0, 1)),
                  pl.BlockSpec((1, s, GROUP_W), lambda b, i: (b, 0, 2))],
        out_specs=pl.BlockSpec((1, t, GROUP_W), lambda b, i: (b, i, 0)),
        out_shape=jax.ShapeDtypeStruct((bsz, s, GROUP_W), F32),
        scratch_shapes=[pltpu.VMEM((t, GROUP_W), F32), pltpu.VMEM((N_HEADS, t, LANE), F32),
                        pltpu.VMEM((N_HEADS, t, t), F32), pltpu.VMEM((t, N_HEADS * t), BF16)],
        compiler_params=_params(("parallel", "arbitrary")),
        name="stickbreak",
    )(pd, pd, pd)


def _row(v, width=None):
    v = v.reshape(1, -1).astype(F32)
    if width is not None and v.shape[1] < width:
        v = jnp.pad(v, ((0, 0), (0, width - v.shape[1])))
    return v


def _pad_cols(w, width):
    return jnp.pad(w, ((0, 0), (0, width - w.shape[1])))


def _pad_rows_at(w, offset, total):
    return jnp.pad(w, ((offset, total - offset - w.shape[0]), (0, 0)))


def _mixer(h2, bsz, s, gain, w_in, ssm_conv_w, ssm_conv_b, ssm_dt_bias, ssm_a_log, ssm_d, ssm_norm,
           rwkv_mu, rwkv_w0, rwkv_w_up, rwkv_a0, rwkv_a_up, rwkv_g_up, rwkv_k_k, rwkv_k_a, rwkv_r_k,
           rwkv_ln_w, rwkv_ln_b, gdn_conv_w, gdn_a_log, gdn_dt_bias, gdn_norm, w_out, tm):
    gw = GROUP_W
    ssm_in = 4 * gw + N_HEADS
    c0 = ssm_in
    c1 = c0 + RWKV_W
    c2 = c1 + 4 * gw + 2 * N_HEADS
    wa = _pad_cols(w_in[:, :c0], SSD_W).astype(BF16)
    wb = w_in[:, c0:c1].astype(BF16)
    wc = _pad_cols(w_in[:, c1:c2], GDN_W).astype(BF16)
    wd = w_in[:, c2:].astype(BF16)
    pa, pb, pc, pd = _inproj(h2, _row(gain), [wa, wb, wc, wd], tm)

    ya = _mixer_call(_ssd_kernel, "ssd", pa.reshape(bsz, s, SSD_W),
                     [ssm_conv_w.astype(F32), _row(ssm_conv_b), _row(ssm_dt_bias, LANE),
                      _row(ssm_a_log, LANE), _row(jnp.repeat(ssm_d, HEAD_DIM)), _row(ssm_norm)])

    lora_w = RWKV_W - 3 * gw
    yb = _mixer_call(_rwkv_kernel, "rwkv", pb.reshape(bsz, s, RWKV_W),
                     [_row(rwkv_mu), _row(rwkv_w0),
                      _pad_rows_at(rwkv_w_up, 0, lora_w).astype(BF16), _row(rwkv_a0),
                      _pad_rows_at(rwkv_a_up, rwkv_w_up.shape[0], lora_w).astype(BF16),
                      _pad_rows_at(rwkv_g_up, rwkv_w_up.shape[0] + rwkv_a_up.shape[0], lora_w).astype(BF16),
                      _row(rwkv_k_k), _row(rwkv_k_a), _row(rwkv_r_k), _row(rwkv_ln_w), _row(rwkv_ln_b)])

    pad4 = lambda t: jnp.pad(t.reshape(1, -1).astype(F32), ((0, 0), (N_HEADS, LANE - 2 * N_HEADS)))
    yc = _mixer_call(_gdn_kernel, "gdn", pc.reshape(bsz, s, GDN_W),
                     [gdn_conv_w.astype(F32), pad4(gdn_a_log), pad4(gdn_dt_bias),
                      _row(jnp.tile(gdn_norm, N_HEADS))])

    yd = _sb(pd.reshape(bsz, s, 3 * gw))

    n = bsz * s
    return _outproj(h2, [ya.reshape(n, gw), yb.reshape(n, gw), yc.reshape(n, gw), yd.reshape(n, gw)],
                    w_out.astype(BF16), tm)


def kernel(x, p, ffn1_norm, ffn1_w_gate, ffn1_w_up, ffn1_w_down, mix_norm, w_in, ssm_conv_w, ssm_conv_b, ssm_dt_bias, ssm_a_log, ssm_d, ssm_norm, rwkv_mu, rwkv_w0, rwkv_w_up, rwkv_a0, rwkv_a_up, rwkv_g_up, rwkv_k_k, rwkv_k_a, rwkv_r_k, rwkv_ln_w, rwkv_ln_b, gdn_conv_w, gdn_a_log, gdn_dt_bias, gdn_norm, w_out, ffn2_norm, ffn2_w_gate, ffn2_w_up, ffn2_w_down, ple_norm, ple_w_gate, ple_w_proj, final_norm):
    bsz, s, d = x.shape
    depth = p.shape[0]
    n = bsz * s
    tm = min(512, n)
    h = x.reshape(n, d)
    for i in range(depth):
        h = _ffn(h, _row(ffn1_norm[i]), ffn1_w_gate[i].astype(BF16), ffn1_w_up[i].astype(BF16),
                 ffn1_w_down[i].astype(BF16), tm)
        h = _mixer(h, bsz, s, mix_norm[i], w_in[i], ssm_conv_w[i], ssm_conv_b[i], ssm_dt_bias[i],
                   ssm_a_log[i], ssm_d[i], ssm_norm[i], rwkv_mu[i], rwkv_w0[i], rwkv_w_up[i],
                   rwkv_a0[i], rwkv_a_up[i], rwkv_g_up[i], rwkv_k_k[i], rwkv_k_a[i], rwkv_r_k[i],
                   rwkv_ln_w[i], rwkv_ln_b[i], gdn_conv_w[i], gdn_a_log[i], gdn_dt_bias[i],
                   gdn_norm[i], w_out[i], tm)
        h = _ffn(h, _row(ffn2_norm[i]), ffn2_w_gate[i].astype(BF16), ffn2_w_up[i].astype(BF16),
                 ffn2_w_down[i].astype(BF16), tm)
        h = _ple(h, p[i].reshape(n, -1), _row(ple_norm[i]), ple_w_gate[i].astype(BF16),
                 ple_w_proj[i].astype(BF16), _row(final_norm), i == depth - 1, tm)
    return h.reshape(bsz, s, d)
```

```python
import functools

import numpy as np
import jax
import jax.numpy as jnp
from jax import lax
from jax.experimental import pallas as pl
from jax.experimental.pallas import tpu as pltpu

F32 = jnp.float32
BF16 = jnp.bfloat16

HEAD_DIM = 64
N_HEADS = 4
GROUP_W = HEAD_DIM * N_HEADS
CHUNK = 64
N_CHUNKS = 4
CONV_K = 4
SSM_STATE = 128
NORM_EPS = 1e-6
L2_EPS = 1e-6
RWKV_GN_EPS = 64e-5
RWKV_DECAY_OFFSET = 0.5
HALO = 8
LANE = 128
VMEM_LIMIT = 56 * 1024 * 1024


def _dg(a, b, ca, cb):
    return lax.dot_general(a, b, (((ca,), (cb,)), ((), ())), preferred_element_type=F32)


_DIMS = {"nn": (1, 0), "nt": (1, 1), "tn": (0, 0)}


def _mm(a, b, kind="nn"):
    ca, cb = _DIMS[kind]
    return _dg(a.astype(BF16), b.astype(BF16), ca, cb)


def _split2(x):
    hi = x.astype(BF16)
    lo = (x - hi.astype(F32)).astype(BF16)
    return hi, lo


def _split3(x):
    hi = x.astype(BF16)
    r = x - hi.astype(F32)
    mid = r.astype(BF16)
    lo = (r - mid.astype(F32)).astype(BF16)
    return hi, mid, lo


def _mm_sel_l(sel, b, kind="nn"):
    ca, cb = _DIMS[kind]
    h, m, l = _split3(b)
    return _dg(sel, h, ca, cb) + (_dg(sel, m, ca, cb) + _dg(sel, l, ca, cb))


def _mm_sel_r(a, sel, kind="nn"):
    ca, cb = _DIMS[kind]
    h, m, l = _split3(a)
    return _dg(h, sel, ca, cb) + (_dg(m, sel, ca, cb) + _dg(l, sel, ca, cb))


def _mm_sel_r2(a, sel):
    h, l = _split2(a)
    return _dg(h, sel, 1, 0) + _dg(l, sel, 1, 0)


def _sigmoid(x):
    return 1.0 / (1.0 + jnp.exp(-x))


def _silu(x):
    return x * _sigmoid(x)


def _softplus(x):
    return jnp.maximum(x, 0.0) + jnp.log1p(jnp.exp(-jnp.abs(x)))


def _iota(shape, dim):
    return lax.broadcasted_iota(jnp.int32, shape, dim)


def _tile_rows(x, n):
    return jnp.concatenate([x] * n, axis=0)


def _rms(x, gain_row):
    ms = jnp.mean(x * x, axis=-1, keepdims=True)
    return x * lax.rsqrt(ms + NORM_EPS) * gain_row


BDF_SAME, BDF_GRP_STATE = range(2)
BDB_SAME, BDB_EYE, BDB_M8, BDB_OFF8, BDB_OFF16, BDB_OFF32, BDB_GRP_ROWS = range(7)
CAT_NEG, CAT_STRICT, CAT_INCL = range(3)
SEL_ONES_H, SEL_ONES_G, SEL_LTRI = range(3)


def _np_consts():
    r = np.arange(GROUP_W)[:, None]
    c = np.arange(GROUP_W)[None, :]
    same = (r // HEAD_DIM) == (c // HEAD_DIM)

    def off(s):
        return ((r // (2 * s)) == (c // (2 * s))) & (((r // s) % 2) == 1) & (((c // s) % 2) == 0)

    bdf = np.stack([same, (r // SSM_STATE) == ((c // HEAD_DIM) // 2)]).astype(np.float32)
    bdb = np.stack([same, r == c, (r // 8) == (c // 8), off(8), off(16), off(32),
                    ((r // HEAD_DIM) // 2) == (c // SSM_STATE)]).astype(np.float32)
    l = np.arange(CHUNK)[:, None]
    s = np.arange(GROUP_W)[None, :] % CHUNK
    cat = np.stack([np.where(l >= s, 0.0, -np.inf), l > s, l >= s]).astype(np.float32)
    t = np.arange(N_CHUNKS * CHUNK)
    ltri = (t[:, None] >= t[None, :]) & ((t[:, None] // CHUNK) == (t[None, :] // CHUNK))
    sel = np.stack([same, (r // SSM_STATE) == (c // SSM_STATE), ltri]).astype(np.float32)
    er = np.arange(LANE)[:, None]
    ec = np.arange(GROUP_W)[None, :] // HEAD_DIM
    expand = np.stack([er == ec, er == ec + N_HEADS]).astype(np.float32)
    pick = np.broadcast_to(np.arange(GROUP_W)[None, :] % HEAD_DIM == 0, (CHUNK, GROUP_W)).astype(np.float32)
    return (jnp.asarray(bdf), jnp.asarray(bdb, BF16), jnp.asarray(cat), jnp.asarray(sel, BF16),
            jnp.asarray(expand, BF16), jnp.asarray(pick, BF16))


def _rows(x_bf, mask_bf):
    return _tile_rows(x_bf, N_HEADS) * mask_bf


def _row_form(colvals, bdb_ref, pick_ref):
    same = bdb_ref[BDB_SAME]
    pick = pick_ref[...]
    parts = [_split3(x) for x in colvals]
    prods = [[_dg(pick, _rows(p, same), 1, 1) for p in ps] for ps in parts]
    return [h + (m + l) for h, m, l in prods]


def _collapse(rows):
    out = rows[0:CHUNK]
    for h in range(1, N_HEADS):
        out = out + rows[h * CHUNK:(h + 1) * CHUNK]
    return out


def _inv_unit_lower(n_cats_bf, bdb_ref):
    tiled = [_tile_rows(n, N_HEADS) for n in n_cats_bf]
    eye = bdb_ref[BDB_EYE]
    m8 = bdb_ref[BDB_M8]
    d = [x * m8 for x in tiled]
    d2 = [_dg(x, x, 1, 0).astype(BF16) for x in d]
    d4 = [_dg(x, x, 1, 0).astype(BF16) for x in d2]
    t = [_dg(eye - x, eye + y, 1, 0) for x, y in zip(d, d2)]
    t = [_dg(x.astype(BF16), eye + y, 1, 0) for x, y in zip(t, d4)]
    for idx in (BDB_OFF8, BDB_OFF16, BDB_OFF32):
        off = bdb_ref[idx]
        tb = [x.astype(BF16) for x in t]
        to = [_dg(x, y * off, 1, 0).astype(BF16) for x, y in zip(tb, tiled)]
        t = [x - _dg(y, z, 1, 0) for x, y, z in zip(t, to, tb)]
    return [x.astype(BF16) for x in t]


def _causal_conv(cur, halo, w_ref, first):
    halo = jnp.where(first, 0.0, halo)
    ext = jnp.concatenate([halo, cur], axis=0)
    acc = cur * w_ref[CONV_K - 1:CONV_K, :]
    for j in range(CONV_K - 1):
        sh = pltpu.roll(ext, CONV_K - 1 - j, axis=0)[HALO:]
        acc = acc + sh * w_ref[j:j + 1, :]
    return acc


def _chunk_rows(c):
    return slice(c * CHUNK, (c + 1) * CHUNK)


def _ffn_kernel(h_ref, g_ref, wg_ref, wu_ref, wd_ref, o_ref):
    x = h_ref[...]
    u = _rms(x, g_ref[...]).astype(BF16)
    a = jnp.dot(u, wg_ref[...], preferred_element_type=F32)
    b = jnp.dot(u, wu_ref[...], preferred_element_type=F32)
    act = (_silu(a) * b).astype(BF16)
    y = jnp.dot(act, wd_ref[...], preferred_element_type=F32)
    o_ref[...] = x + 0.5 * y


def _const_spec(shape):
    nd = len(shape)
    return pl.BlockSpec(shape, lambda *_: (0,) * nd)


def _params(sem):
    return pltpu.CompilerParams(dimension_semantics=sem, vmem_limit_bytes=VMEM_LIMIT)


def _ffn(h, gain, wg, wu, wd, tm):
    n, d = h.shape
    f = wg.shape[1]
    return pl.pallas_call(
        _ffn_kernel,
        grid=(n // tm,),
        in_specs=[pl.BlockSpec((tm, d), lambda i: (i, 0)), _const_spec((1, d)),
                  _const_spec((d, f)), _const_spec((d, f)), _const_spec((f, d))],
        out_specs=pl.BlockSpec((tm, d), lambda i: (i, 0)),
        out_shape=jax.ShapeDtypeStruct((n, d), F32),
        compiler_params=_params(("parallel",)),
        name="ffn",
    )(h, gain, wg, wu, wd)


def _inproj_kernel(h_ref, g_ref, wa_ref, wb_ref, wc_ref, wd_ref, oa_ref, ob_ref, oc_ref, od_ref):
    u = _rms(h_ref[...], g_ref[...]).astype(BF16)
    for w_ref, o_ref in ((wa_ref, oa_ref), (wb_ref, ob_ref), (wc_ref, oc_ref), (wd_ref, od_ref)):
        o_ref[...] = jnp.dot(u, w_ref[...], preferred_element_type=F32).astype(o_ref.dtype)


def _inproj(h, gain, ws, tm):
    n, d = h.shape
    widths = [w.shape[1] for w in ws]
    dtypes = [F32, F32, F32, BF16]
    return pl.pallas_call(
        _inproj_kernel,
        grid=(n // tm,),
        in_specs=[pl.BlockSpec((tm, d), lambda i: (i, 0)), _const_spec((1, d))]
                 + [_const_spec((d, wd)) for wd in widths],
        out_specs=[pl.BlockSpec((tm, wd), lambda i: (i, 0)) for wd in widths],
        out_shape=[jax.ShapeDtypeStruct((n, wd), dt) for wd, dt in zip(widths, dtypes)],
        compiler_params=_params(("parallel",)),
        name="inproj",
    )(h, gain, *ws)


def _outproj_kernel(h_ref, ya_ref, yb_ref, yc_ref, yd_ref, w_ref, o_ref):
    acc = h_ref[...]
    for i, y_ref in enumerate((ya_ref, yb_ref, yc_ref, yd_ref)):
        acc = acc + jnp.dot(y_ref[...].astype(BF16), w_ref[i * GROUP_W:(i + 1) * GROUP_W, :],
                            preferred_element_type=F32)
    o_ref[...] = acc


def _outproj(h, ys, w, tm):
    n, d = h.shape
    return pl.pallas_call(
        _outproj_kernel,
        grid=(n // tm,),
        in_specs=[pl.BlockSpec((tm, d), lambda i: (i, 0))]
                 + [pl.BlockSpec((tm, GROUP_W), lambda i: (i, 0)) for _ in ys]
                 + [_const_spec(w.shape)],
        out_specs=pl.BlockSpec((tm, d), lambda i: (i, 0)),
        out_shape=jax.ShapeDtypeStruct((n, d), F32),
        compiler_params=_params(("parallel",)),
        name="outproj",
    )(h, *ys, w)


def _ple_kernel(h_ref, p_ref, g_ref, wg_ref, wp_ref, fg_ref, o_ref, *, final):
    x = h_ref[...]
    u = _rms(x, g_ref[...]).astype(BF16)
    gate = _sigmoid(jnp.dot(u, wg_ref[...], preferred_element_type=F32))
    e = jnp.dot(p_ref[...].astype(BF16), wp_ref[...], preferred_element_type=F32)
    y = x + e * gate
    if final:
        y = _rms(y, fg_ref[...])
    o_ref[...] = y


def _ple(h, p, gain, wg, wp, fgain, final, tm):
    n, d = h.shape
    pd = p.shape[1]
    return pl.pallas_call(
        functools.partial(_ple_kernel, final=final),
        grid=(n // tm,),
        in_specs=[pl.BlockSpec((tm, d), lambda i: (i, 0)), pl.BlockSpec((tm, pd), lambda i: (i, 0)),
                  _const_spec((1, d)), _const_spec((d, d)), _const_spec((pd, d)), _const_spec((1, d))],
        out_specs=pl.BlockSpec((tm, d), lambda i: (i, 0)),
        out_shape=jax.ShapeDtypeStruct((n, d), F32),
        compiler_params=_params(("parallel",)),
        name="ple",
    )(h, p, gain, wg, wp, fgain)


def _halo_map(rows_per_block):
    step = rows_per_block // HALO
    return lambda b, i: (b, jnp.maximum(i * step - 1, 0), 0)


def _mixer_call(kernel_fn, name, x, params):
    bsz, s, w = x.shape
    t = N_CHUNKS * CHUNK
    consts = _np_consts()
    return pl.pallas_call(
        kernel_fn,
        grid=(bsz, s // t),
        in_specs=[pl.BlockSpec((1, t, w), lambda b, i: (b, i, 0)),
                  pl.BlockSpec((1, HALO, w), _halo_map(t))]
                 + [_const_spec(c.shape) for c in consts]
                 + [_const_spec(p.shape) for p in params],
        out_specs=pl.BlockSpec((1, t, GROUP_W), lambda b, i: (b, i, 0)),
        out_shape=jax.ShapeDtypeStruct((bsz, s, GROUP_W), F32),
        scratch_shapes=[pltpu.VMEM((GROUP_W, GROUP_W), F32)],
        compiler_params=_params(("parallel", "arbitrary")),
        name=name,
    )(x, x, *consts, *params)


def _reset_state(st_ref):
    @pl.when(pl.program_id(1) == 0)
    def _():
        st_ref[...] = jnp.zeros_like(st_ref)


SSD_W = 1152


def _ssd_kernel(x_ref, halo_ref, bdf_ref, bdb_ref, cat_ref, sel_ref, exp_ref, pick_ref,
                cw_ref, cb_ref, dtb_ref, alog_ref, dsk_ref, nw_ref, o_ref, st_ref):
    _reset_state(st_ref)
    first = pl.program_id(1) == 0
    x = x_ref[0]
    z = x[:, 0:GROUP_W]
    xbc = _causal_conv(x[:, GROUP_W:4 * GROUP_W], halo_ref[0][:, GROUP_W:4 * GROUP_W], cw_ref, first)
    xbc = _silu(xbc + cb_ref[...])
    xs = xbc[:, 0:GROUP_W]
    bm = xbc[:, GROUP_W:2 * GROUP_W].astype(BF16)
    cm = xbc[:, 2 * GROUP_W:3 * GROUP_W].astype(BF16)
    dt_pad = _softplus(x[:, 4 * GROUP_W:] + dtb_ref[...])
    la_pad = dt_pad * (-jnp.exp(alog_ref[...]))
    dt_b = _mm_sel_r(dt_pad, exp_ref[0])
    la_b = _mm_sel_r(la_pad, exp_ref[0])
    cs = _mm_sel_l(sel_ref[SEL_LTRI], la_b)
    xc = xs * dt_b
    ecs = jnp.exp(cs)
    same = bdb_ref[BDB_SAME]
    grp_rows = bdb_ref[BDB_GRP_ROWS]
    grp_state = bdf_ref[BDF_GRP_STATE]
    chunks = [_chunk_rows(c) for c in range(N_CHUNKS)]
    neg = cat_ref[CAT_NEG]
    rforms = _row_form([cs[rs] for rs in chunks], bdb_ref, pick_ref)
    segs = [jnp.exp(cs[rs] - rf + neg) for rs, rf in zip(chunks, rforms)]
    scores = [(_dg(cm[rs], _rows(bm[rs], grp_rows), 1, 1) * sg).astype(BF16)
              for rs, sg in zip(chunks, segs)]
    xc_bf = xc.astype(BF16)
    ys = [_dg(sc, _rows(xc_bf[rs], same), 1, 0) for rs, sc in zip(chunks, scores)]
    lasts = [cs[rs][CHUNK - 1:CHUNK, :] for rs in chunks]
    upds = [_dg(bm[rs], (xc[rs] * jnp.exp(last - cs[rs])).astype(BF16), 0, 0) * grp_state
            for rs, last in zip(chunks, lasts)]
    st = st_ref[...]
    for c, rs in enumerate(chunks):
        ys[c] = ys[c] + _dg(cm[rs], st.astype(BF16), 1, 0) * ecs[rs]
        st = st * jnp.exp(lasts[c]) + upds[c]
    st_ref[...] = st
    y = (jnp.concatenate(ys, axis=0) + dsk_ref[...] * xs) * _silu(z)
    ms = _mm_sel_r2(y * y, sel_ref[SEL_ONES_G]) * (1.0 / SSM_STATE)
    o_ref[0] = y * lax.rsqrt(ms + NORM_EPS) * nw_ref[...]


GDN_W = 1152


def _gdn_kernel(x_ref, halo_ref, bdf_ref, bdb_ref, cat_ref, sel_ref, exp_ref, pick_ref,
                cw_ref, alog_ref, dtb_ref, nw_ref, o_ref, st_ref):
    _reset_state(st_ref)
    first = pl.program_id(1) == 0
    ones_h = sel_ref[SEL_ONES_H]
    x = x_ref[0]
    qkv = _silu(_causal_conv(x[:, 0:3 * GROUP_W], halo_ref[0][:, 0:3 * GROUP_W], cw_ref, first))
    z = x[:, 3 * GROUP_W:4 * GROUP_W]
    ba = x[:, 4 * GROUP_W:]
    q = qkv[:, 0:GROUP_W]
    k = qkv[:, GROUP_W:2 * GROUP_W]
    v = qkv[:, 2 * GROUP_W:3 * GROUP_W]
    q = q * lax.rsqrt(_mm_sel_r2(q * q, ones_h) + L2_EPS) * (HEAD_DIM ** -0.5)
    k = k * lax.rsqrt(_mm_sel_r2(k * k, ones_h) + L2_EPS)
    beta_pad = _sigmoid(ba)
    g_pad = -jnp.exp(alog_ref[...]) * _softplus(ba + dtb_ref[...])
    beta_b = _mm_sel_r(beta_pad, exp_ref[0])
    g_b = _mm_sel_r(g_pad, exp_ref[1])
    gc = _mm_sel_l(sel_ref[SEL_LTRI], g_b)
    eg = jnp.exp(gc)
    kb = k * beta_b
    q_bf = q.astype(BF16)
    k_bf = k.astype(BF16)
    kb_bf = kb.astype(BF16)
    vb_bf = (v * beta_b).astype(BF16)
    kbg_bf = (kb * eg).astype(BF16)
    qg_bf = (q * eg).astype(BF16)
    same = bdb_ref[BDB_SAME]
    chunks = [_chunk_rows(c) for c in range(N_CHUNKS)]
    neg = cat_ref[CAT_NEG]
    strict = cat_ref[CAT_STRICT]
    rforms = _row_form([gc[rs] for rs in chunks], bdb_ref, pick_ref)
    decays = [jnp.exp(gc[rs] - rf + neg) for rs, rf in zip(chunks, rforms)]
    k_rows = [_rows(k_bf[rs], same) for rs in chunks]
    a_cats = [(_dg(kb_bf[rs], kr, 1, 1) * dc * strict).astype(BF16)
              for rs, kr, dc in zip(chunks, k_rows, decays)]
    qks = [(_dg(q_bf[rs], kr, 1, 1) * dc).astype(BF16) for rs, kr, dc in zip(chunks, k_rows, decays)]
    t_bds = _inv_unit_lower(a_cats, bdb_ref)
    uws = [_collapse(_dg(t_bd, jnp.concatenate([_rows(vb_bf[rs], same), _rows(kbg_bf[rs], same)], axis=1),
                         1, 0)) for rs, t_bd in zip(chunks, t_bds)]
    lasts = [gc[rs][CHUNK - 1:CHUNK, :] for rs in chunks]
    k_decs = [(k[rs] * jnp.exp(last - gc[rs])).astype(BF16) for rs, last in zip(chunks, lasts)]
    same_f = bdf_ref[BDF_SAME]
    us = [uw[:, 0:GROUP_W] for uw in uws]
    ws_bf = [uw[:, GROUP_W:].astype(BF16) for uw in uws]
    m_st = [(-same_f * _dg(kd, w, 0, 0)).astype(BF16) for kd, w in zip(k_decs, ws_bf)]
    c_st = [same_f * _dg(kd, u.astype(BF16), 0, 0) for kd, u in zip(k_decs, us)]
    st = st_ref[...]
    sts = []
    for c in range(N_CHUNKS):
        st_bf = st.astype(BF16)
        sts.append(st_bf)
        st = st * jnp.exp(lasts[c]) + _dg(m_st[c], st_bf, 1, 0) + c_st[c]
    st_ref[...] = st
    v_news = [(u - _dg(w, s_bf, 1, 0)).astype(BF16) for u, w, s_bf in zip(us, ws_bf, sts)]
    os_ = [_dg(qg_bf[rs], s_bf, 1, 0) + _dg(qk, _rows(vn, same), 1, 0)
           for rs, s_bf, qk, vn in zip(chunks, sts, qks, v_news)]
    o = jnp.concatenate(os_, axis=0)
    ms = _mm_sel_r2(o * o, ones_h) * (1.0 / HEAD_DIM)
    o_ref[0] = o * lax.rsqrt(ms + NORM_EPS) * nw_ref[...] * _silu(z)


RWKV_W = 896


def _rwkv_kernel(x_ref, halo_ref, bdf_ref, bdb_ref, cat_ref, sel_ref, exp_ref, pick_ref,
                 mu_ref, w0_ref, wup_ref, a0_ref, aup_ref, gup_ref, kk_ref, ka_ref,
                 rk_ref, lnw_ref, lnb_ref, o_ref, st_ref):
    _reset_state(st_ref)
    first = pl.program_id(1) == 0
    ones_h = sel_ref[SEL_ONES_H]
    x = x_ref[0]
    halo = jnp.where(first, 0.0, halo_ref[0])
    shifted = pltpu.roll(jnp.concatenate([halo, x], axis=0), 1, axis=0)[HALO:]
    x = x + mu_ref[...] * (shifted - x)
    r = x[:, 0:GROUP_W]
    k = x[:, GROUP_W:2 * GROUP_W]
    v = x[:, 2 * GROUP_W:3 * GROUP_W]
    lora = x[:, 3 * GROUP_W:]
    w_log = -_softplus(-(w0_ref[...] + _mm_sel_r2(jnp.tanh(lora), wup_ref[...]))) - RWKV_DECAY_OFFSET
    lw = -jnp.exp(w_log)
    a_gate = _sigmoid(a0_ref[...] + _mm_sel_r2(lora, aup_ref[...]))
    g = _mm_sel_r2(_sigmoid(lora), gup_ref[...])
    kk = k * kk_ref[...]
    kk = kk * lax.rsqrt(_mm_sel_r2(kk * kk, ones_h) + L2_EPS)
    k = k * (1.0 + (a_gate - 1.0) * ka_ref[...])
    b_v = kk * a_gate
    cl = _mm_sel_l(sel_ref[SEL_LTRI], lw)
    e_in = jnp.exp(cl)
    e_inv = jnp.exp(-cl)
    at_bf = (-kk * jnp.exp(cl - lw)).astype(BF16)
    rt_bf = (r * e_in).astype(BF16)
    bt_bf = (b_v * e_inv).astype(BF16)
    kt_bf = (k * e_inv).astype(BF16)
    v_bf = v.astype(BF16)
    same = bdb_ref[BDB_SAME]
    strict = cat_ref[CAT_STRICT]
    incl = cat_ref[CAT_INCL]
    chunks = [_chunk_rows(c) for c in range(N_CHUNKS)]
    zero = jnp.zeros((), F32)
    b_rows = [_rows(bt_bf[rs], same) for rs in chunks]
    k_rows = [_rows(kt_bf[rs], same) for rs in chunks]
    v_rows = [_rows(v_bf[rs], same) for rs in chunks]
    n_cats = [jnp.where(strict > 0, -_dg(at_bf[rs], br, 1, 1), zero).astype(BF16)
              for rs, br in zip(chunks, b_rows)]
    aks = [jnp.where(strict > 0, _dg(at_bf[rs], kr, 1, 1), zero).astype(BF16)
           for rs, kr in zip(chunks, k_rows)]
    rbs = [jnp.where(incl > 0, _dg(rt_bf[rs], br, 1, 1), zero).astype(BF16)
           for rs, br in zip(chunks, b_rows)]
    rks = [jnp.where(incl > 0, _dg(rt_bf[rs], kr, 1, 1), zero).astype(BF16)
           for rs, kr in zip(chunks, k_rows)]
    t_bds = _inv_unit_lower(n_cats, bdb_ref)
    tas = [_dg(t_bd, _rows(at_bf[rs], same), 1, 0).astype(BF16) for rs, t_bd in zip(chunks, t_bds)]
    tvs = [_dg(t_bd, _rows(_dg(ak, vr, 1, 0).astype(BF16), same), 1, 0)
           for t_bd, ak, vr in zip(t_bds, aks, v_rows)]
    y_vs = [_dg(rk, vr, 1, 0) for rk, vr in zip(rks, v_rows)]
    lasts = [cl[rs][CHUNK - 1:CHUNK, :] for rs in chunks]
    to_ends = [jnp.exp(last - cl[rs]) for rs, last in zip(chunks, lasts)]
    same_f = bdf_ref[BDF_SAME]
    b_end_rows = [_rows((b_v[rs] * te).astype(BF16), same) for rs, te in zip(chunks, to_ends)]
    m_st = [_dg(ta, br, 0, 0).astype(BF16) for ta, br in zip(tas, b_end_rows)]
    c_st = [_dg(tv.astype(BF16), br, 0, 0) + same_f * _dg(v_bf[rs], (k[rs] * te).astype(BF16), 0, 0)
            for tv, br, rs, te in zip(tvs, b_end_rows, chunks, to_ends)]
    st = st_ref[...]
    sts = []
    for c in range(N_CHUNKS):
        st_bf = st.astype(BF16)
        sts.append(st_bf)
        st = st * jnp.exp(lasts[c]) + _dg(st_bf, m_st[c], 1, 0) + c_st[c]
    st_ref[...] = st
    u_rows = [(_dg(ta, s_bf, 1, 1) + tv).astype(BF16) for ta, tv, s_bf in zip(tas, tvs, sts)]
    ys = [_dg(rt_bf[rs], s_bf, 1, 1) + _dg(rb, ur, 1, 0) + yv
          for rs, s_bf, rb, ur, yv in zip(chunks, sts, rbs, u_rows, y_vs)]
    y = jnp.concatenate(ys, axis=0)
    inv_d = 1.0 / HEAD_DIM
    mean = _mm_sel_r2(y, ones_h) * inv_d
    yc = y - mean
    var = _mm_sel_r2(yc * yc, ones_h) * inv_d
    yn = yc * lax.rsqrt(var + RWKV_GN_EPS) * lnw_ref[...] + lnb_ref[...]
    bonus = _mm_sel_r2(r * k * rk_ref[...], ones_h) * v
    o_ref[0] = (yn + bonus) * g


SB_BLOCK = 256
SB_NEGLIGIBLE = 128.0


def _neg_abs(x):
    sign = jnp.uint32(0x80000000)
    return lax.bitcast_convert_type(lax.bitcast_convert_type(x, jnp.uint32) | sign, F32)


def _sb_kernel(q_ref, k_ref, v_ref, o_ref, acc_ref, run_ref, zz_ref, w_ref):
    qi = pl.program_id(1)
    t = SB_BLOCK
    heads = range(N_HEADS)
    scale = HEAD_DIM ** -0.5
    assert scale == 0.125
    lane_head = _iota((t, GROUP_W), 1) // HEAD_DIM
    q = q_ref[0] * jnp.asarray(scale, q_ref.dtype)
    zero = jnp.zeros((), q.dtype)
    q_h = [jnp.where(lane_head == h, q, zero) for h in heads]
    tri = (_iota((t, t), 0) >= _iota((t, t), 1)).astype(BF16)
    upper2 = jnp.concatenate([tri, tri], axis=0)
    causal = _iota((t, t), 1) < _iota((t, t), 0)
    acc_ref[...] = jnp.zeros_like(acc_ref)
    run_ref[...] = jnp.zeros_like(run_ref)

    def scores(j):
        kj = k_ref[0, pl.ds(pl.multiple_of(j * t, t), t), :]
        return [_dg(q_h[h], kj, 1, 1) for h in heads]

    def weights(zz, masked):
        parts, sums = [], []
        for h in heads:
            sp = jnp.maximum(zz[h], 0.0) + jnp.log(1.0 + jnp.exp(_neg_abs(zz[h])))
            if masked:
                sp = jnp.where(causal, sp, 0.0)
            hi, lo = _split2(sp)
            parts.append(jnp.concatenate([hi, lo], axis=1))
            sums.append(jnp.sum(sp, axis=1, keepdims=True))
        later = [_dg(p, upper2, 1, 0) for p in parts]
        wgts = []
        for h in heads:
            run = run_ref[h]
            wgt = jnp.exp(zz[h] - later[h] - jnp.concatenate([run] * (t // LANE), axis=1))
            if masked:
                wgt = jnp.where(causal, wgt, 0.0)
            wgts.append(wgt.astype(BF16))
            run_ref[h] = run + sums[h]
        return jnp.concatenate(wgts, axis=1)

    def accumulate(w_cat, j):
        vj = v_ref[0, pl.ds(pl.multiple_of(j * t, t), t), :]
        v_cat = jnp.concatenate([jnp.where(lane_head == h, vj, zero) for h in heads], axis=0)
        acc_ref[...] += _dg(w_cat, v_cat, 1, 0)

    def stash(zz, w_cat):
        for h in heads:
            zz_ref[h] = zz[h]
        w_ref[...] = w_cat

    zz_next = scores(jnp.maximum(qi - 1, 0))
    stash(zz_next, weights(scores(qi), True))

    def run_min():
        r = run_ref[0]
        for h in range(1, N_HEADS):
            r = jnp.minimum(r, run_ref[h])
        return jnp.min(r)

    def cond(carry):
        i, rmin = carry
        return jnp.logical_and(i <= qi, rmin < SB_NEGLIGIBLE)

    def body(carry):
        i, _ = carry
        j = qi - i
        accumulate(w_ref[...], j + 1)
        zz_next = scores(jnp.maximum(j - 1, 0))
        stash(zz_next, weights([zz_ref[h] for h in heads], False))
        return i + 1, run_min()

    i_end, _ = lax.while_loop(cond, body, (jnp.int32(1), run_min()))
    accumulate(w_ref[...], qi - (i_end - 1))
    o_ref[0] = acc_ref[...]


def _sb(pd):
    bsz, s, _ = pd.shape
    t = SB_BLOCK
    return pl.pallas_call(
        _sb_kernel,
        grid=(bsz, s // t),
        in_specs=[pl.BlockSpec((1, t, GROUP_W), lambda b, i: (b, i, 0)),
                  pl.BlockSpec((1, s, GROUP_W), lambda b, i: (b, 0, 1)),
                  pl.BlockSpec((1, s, GROUP_W), lambda b, i: (b, 0, 2))],
        out_specs=pl.BlockSpec((1, t, GROUP_W), lambda b, i: (b, i, 0)),
        out_shape=jax.ShapeDtypeStruct((bsz, s, GROUP_W), F32),
        scratch_shapes=[pltpu.VMEM((t, GROUP_W), F32), pltpu.VMEM((N_HEADS, t, LANE), F32),
                        pltpu.VMEM((N_HEADS, t, t), F32), pltpu.VMEM((t, N_HEADS * t), BF16)],
        compiler_params=_params(("parallel", "arbitrary")),
        name="stickbreak",
    )(pd, pd, pd)


def _row(v, width=None):
    v = v.reshape(1, -1).astype(F32)
    if width is not None and v.shape[1] < width:
        v = jnp.pad(v, ((0, 0), (0, width - v.shape[1])))
    return v


def _pad_cols(w, width):
    return jnp.pad(w, ((0, 0), (0, width - w.shape[1])))


def _pad_rows_at(w, offset, total):
    return jnp.pad(w, ((offset, total - offset - w.shape[0]), (0, 0)))


def _mixer(h2, bsz, s, gain, w_in, ssm_conv_w, ssm_conv_b, ssm_dt_bias, ssm_a_log, ssm_d, ssm_norm,
           rwkv_mu, rwkv_w0, rwkv_w_up, rwkv_a0, rwkv_a_up, rwkv_g_up, rwkv_k_k, rwkv_k_a, rwkv_r_k,
           rwkv_ln_w, rwkv_ln_b, gdn_conv_w, gdn_a_log, gdn_dt_bias, gdn_norm, w_out, tm):
    gw = GROUP_W
    ssm_in = 4 * gw + N_HEADS
    c0 = ssm_in
    c1 = c0 + RWKV_W
    c2 = c1 + 4 * gw + 2 * N_HEADS
    wa = _pad_cols(w_in[:, :c0], SSD_W).astype(BF16)
    wb = w_in[:, c0:c1].astype(BF16)
    wc = _pad_cols(w_in[:, c1:c2], GDN_W).astype(BF16)
    wd = w_in[:, c2:].astype(BF16)
    pa, pb, pc, pd = _inproj(h2, _row(gain), [wa, wb, wc, wd], tm)

    ya = _mixer_call(_ssd_kernel, "ssd", pa.reshape(bsz, s, SSD_W),
                     [ssm_conv_w.astype(F32), _row(ssm_conv_b), _row(ssm_dt_bias, LANE),
                      _row(ssm_a_log, LANE), _row(jnp.repeat(ssm_d, HEAD_DIM)), _row(ssm_norm)])

    lora_w = RWKV_W - 3 * gw
    yb = _mixer_call(_rwkv_kernel, "rwkv", pb.reshape(bsz, s, RWKV_W),
                     [_row(rwkv_mu), _row(rwkv_w0),
                      _pad_rows_at(rwkv_w_up, 0, lora_w).astype(BF16), _row(rwkv_a0),
                      _pad_rows_at(rwkv_a_up, rwkv_w_up.shape[0], lora_w).astype(BF16),
                      _pad_rows_at(rwkv_g_up, rwkv_w_up.shape[0] + rwkv_a_up.shape[0], lora_w).astype(BF16),
                      _row(rwkv_k_k), _row(rwkv_k_a), _row(rwkv_r_k), _row(rwkv_ln_w), _row(rwkv_ln_b)])

    pad4 = lambda t: jnp.pad(t.reshape(1, -1).astype(F32), ((0, 0), (N_HEADS, LANE - 2 * N_HEADS)))
    yc = _mixer_call(_gdn_kernel, "gdn", pc.reshape(bsz, s, GDN_W),
                     [gdn_conv_w.astype(F32), pad4(gdn_a_log), pad4(gdn_dt_bias),
                      _row(jnp.tile(gdn_norm, N_HEADS))])

    yd = _sb(pd.reshape(bsz, s, 3 * gw))

    n = bsz * s
    return _outproj(h2, [ya.reshape(n, gw), yb.reshape(n, gw), yc.reshape(n, gw), yd.reshape(n, gw)],
                    w_out.astype(BF16), tm)


def kernel(x, p, ffn1_norm, ffn1_w_gate, ffn1_w_up, ffn1_w_down, mix_norm, w_in, ssm_conv_w, ssm_conv_b, ssm_dt_bias, ssm_a_log, ssm_d, ssm_norm, rwkv_mu, rwkv_w0, rwkv_w_up, rwkv_a0, rwkv_a_up, rwkv_g_up, rwkv_k_k, rwkv_k_a, rwkv_r_k, rwkv_ln_w, rwkv_ln_b, gdn_conv_w, gdn_a_log, gdn_dt_bias, gdn_norm, w_out, ffn2_norm, ffn2_w_gate, ffn2_w_up, ffn2_w_down, ple_norm, ple_w_gate, ple_w_proj, final_norm):
    bsz, s, d = x.shape
    depth = p.shape[0]
    n = bsz * s
    tm = min(512, n)
    h = x.reshape(n, d)
    for i in range(depth):
        h = _ffn(h, _row(ffn1_norm[i]), ffn1_w_gate[i].astype(BF16), ffn1_w_up[i].astype(BF16),
                 ffn1_w_down[i].astype(BF16), tm)
        h = _mixer(h, bsz, s, mix_norm[i], w_in[i], ssm_conv_w[i], ssm_conv_b[i], ssm_dt_bias[i],
                   ssm_a_log[i], ssm_d[i], ssm_norm[i], rwkv_mu[i], rwkv_w0[i], rwkv_w_up[i],
                   rwkv_a0[i], rwkv_a_up[i], rwkv_g_up[i], rwkv_k_k[i], rwkv_k_a[i], rwkv_r_k[i],
                   rwkv_ln_w[i], rwkv_ln_b[i], gdn_conv_w[i], gdn_a_log[i], gdn_dt_bias[i],
                   gdn_norm[i], w_out[i], tm)
        h = _ffn(h, _row(ffn2_norm[i]), ffn2_w_gate[i].astype(BF16), ffn2_w_up[i].astype(BF16),
                 ffn2_w_down[i].astype(BF16), tm)
        h = _ple(h, p[i].reshape(n, -1), _row(ple_norm[i]), ple_w_gate[i].astype(BF16),
                 ple_w_proj[i].astype(BF16), _row(final_norm), i == depth - 1, tm)
    return h.reshape(bsz, s, d)
```

```python
import functools

import numpy as np
import jax
import jax.numpy as jnp
from jax import lax
from jax.experimental import pallas as pl
from jax.experimental.pallas import tpu as pltpu

F32 = jnp.float32
BF16 = jnp.bfloat16

HEAD_DIM = 64
N_HEADS = 4
GROUP_W = HEAD_DIM * N_HEADS
CHUNK = 64
N_CHUNKS = 4
CONV_K = 4
SSM_STATE = 128
NORM_EPS = 1e-6
L2_EPS = 1e-6
RWKV_GN_EPS = 64e-5
RWKV_DECAY_OFFSET = 0.5
HALO = 8
LANE = 128
VMEM_LIMIT = 56 * 1024 * 1024


def _dg(a, b, ca, cb):
    return lax.dot_general(a, b, (((ca,), (cb,)), ((), ())), preferred_element_type=F32)


_DIMS = {"nn": (1, 0), "nt": (1, 1), "tn": (0, 0)}


def _mm(a, b, kind="nn"):
    ca, cb = _DIMS[kind]
    return _dg(a.astype(BF16), b.astype(BF16), ca, cb)


def _split2(x):
    hi = x.astype(BF16)
    lo = (x - hi.astype(F32)).astype(BF16)
    return hi, lo


def _split3(x):
    hi = x.astype(BF16)
    r = x - hi.astype(F32)
    mid = r.astype(BF16)
    lo = (r - mid.astype(F32)).astype(BF16)
    return hi, mid, lo


def _mm_sel_l(sel, b, kind="nn"):
    ca, cb = _DIMS[kind]
    h, m, l = _split3(b)
    return _dg(sel, h, ca, cb) + (_dg(sel, m, ca, cb) + _dg(sel, l, ca, cb))


def _mm_sel_r(a, sel, kind="nn"):
    ca, cb = _DIMS[kind]
    h, m, l = _split3(a)
    return _dg(h, sel, ca, cb) + (_dg(m, sel, ca, cb) + _dg(l, sel, ca, cb))


def _mm_sel_r2(a, sel):
    h, l = _split2(a)
    return _dg(h, sel, 1, 0) + _dg(l, sel, 1, 0)


def _sigmoid(x):
    return 1.0 / (1.0 + jnp.exp(-x))


def _silu(x):
    return x * _sigmoid(x)


def _softplus(x):
    return jnp.maximum(x, 0.0) + jnp.log1p(jnp.exp(-jnp.abs(x)))


def _iota(shape, dim):
    return lax.broadcasted_iota(jnp.int32, shape, dim)


def _tile_rows(x, n):
    return jnp.concatenate([x] * n, axis=0)


def _rms(x, gain_row):
    ms = jnp.mean(x * x, axis=-1, keepdims=True)
    return x * lax.rsqrt(ms + NORM_EPS) * gain_row


BDF_SAME, BDF_GRP_STATE = range(2)
BDB_SAME, BDB_GRP_ROWS = range(2)
CAT_NEG, CAT_STRICT, CAT_INCL = range(3)
CATB_EYE, CATB_M8, CATB_OFF8, CATB_OFF16, CATB_OFF32 = range(5)
SEL_ONES_H, SEL_ONES_G, SEL_LTRI = range(3)


def _np_consts():
    r = np.arange(GROUP_W)[:, None]
    c = np.arange(GROUP_W)[None, :]
    same = (r // HEAD_DIM) == (c // HEAD_DIM)

    bdf = np.stack([same, (r // SSM_STATE) == ((c // HEAD_DIM) // 2)]).astype(np.float32)
    bdb = np.stack([same, ((r // HEAD_DIM) // 2) == (c // SSM_STATE)]).astype(np.float32)
    l = np.arange(CHUNK)[:, None]
    s = np.arange(GROUP_W)[None, :] % CHUNK
    cat = np.stack([np.where(l >= s, 0.0, -np.inf), l > s, l >= s]).astype(np.float32)

    def off(b):
        return ((l // (2 * b)) == (s // (2 * b))) & (((l // b) % 2) == 1) & (((s // b) % 2) == 0)

    catb = np.stack([l == s, (l // 8) == (s // 8), off(8), off(16), off(32)]).astype(np.float32)
    t = np.arange(N_CHUNKS * CHUNK)
    ltri = (t[:, None] >= t[None, :]) & ((t[:, None] // CHUNK) == (t[None, :] // CHUNK))
    sel = np.stack([same, (r // SSM_STATE) == (c // SSM_STATE), ltri]).astype(np.float32)
    er = np.arange(LANE)[:, None]
    ec = np.arange(GROUP_W)[None, :] // HEAD_DIM
    expand = np.stack([er == ec, er == ec + N_HEADS]).astype(np.float32)
    pick = np.broadcast_to(np.arange(GROUP_W)[None, :] % HEAD_DIM == 0, (CHUNK, GROUP_W)).astype(np.float32)
    return (jnp.asarray(bdf), jnp.asarray(bdb, BF16), jnp.asarray(cat), jnp.asarray(sel, BF16),
            jnp.asarray(expand, BF16), jnp.asarray(pick, BF16), jnp.asarray(catb, BF16))


def _rows(x_bf, mask_bf):
    return _tile_rows(x_bf, N_HEADS) * mask_bf


def _row_form(colvals, bdb_ref, pick_ref):
    same = bdb_ref[BDB_SAME]
    pick = pick_ref[...]
    parts = [_split3(x) for x in colvals]
    prods = [[_dg(pick, _rows(p, same), 1, 1) for p in ps] for ps in parts]
    return [h + (m + l) for h, m, l in prods]


def _inv_unit_lower(n_cats_bf, catb_ref, same):
    def mul(a_cat_bf, b_cat_bf):
        return _dg(a_cat_bf, _rows(b_cat_bf, same), 1, 0)

    eye = catb_ref[CATB_EYE]
    m8 = catb_ref[CATB_M8]
    d = [n * m8 for n in n_cats_bf]
    d2 = [mul(x, x).astype(BF16) for x in d]
    d4 = [mul(x, x).astype(BF16) for x in d2]
    t = [mul(eye - x, eye + y) for x, y in zip(d, d2)]
    t = [mul(x.astype(BF16), eye + y) for x, y in zip(t, d4)]
    for idx in (CATB_OFF8, CATB_OFF16, CATB_OFF32):
        off = catb_ref[idx]
        tb = [x.astype(BF16) for x in t]
        to = [mul(x, n * off).astype(BF16) for x, n in zip(tb, n_cats_bf)]
        t = [x - mul(y, z) for x, y, z in zip(t, to, tb)]
    return [x.astype(BF16) for x in t]


def _causal_conv(cur, halo, w_ref, first):
    halo = jnp.where(first, 0.0, halo)
    ext = jnp.concatenate([halo, cur], axis=0)
    acc = cur * w_ref[CONV_K - 1:CONV_K, :]
    for j in range(CONV_K - 1):
        sh = pltpu.roll(ext, CONV_K - 1 - j, axis=0)[HALO:]
        acc = acc + sh * w_ref[j:j + 1, :]
    return acc


def _chunk_rows(c):
    return slice(c * CHUNK, (c + 1) * CHUNK)


def _ffn_kernel(h_ref, g_ref, wg_ref, wu_ref, wd_ref, o_ref):
    x = h_ref[...]
    u = _rms(x, g_ref[...]).astype(BF16)
    a = jnp.dot(u, wg_ref[...], preferred_element_type=F32)
    b = jnp.dot(u, wu_ref[...], preferred_element_type=F32)
    act = (_silu(a) * b).astype(BF16)
    y = jnp.dot(act, wd_ref[...], preferred_element_type=F32)
    o_ref[...] = x + 0.5 * y


def _const_spec(shape):
    nd = len(shape)
    return pl.BlockSpec(shape, lambda *_: (0,) * nd)


def _params(sem):
    return pltpu.CompilerParams(dimension_semantics=sem, vmem_limit_bytes=VMEM_LIMIT)


def _ffn(h, gain, wg, wu, wd, tm):
    n, d = h.shape
    f = wg.shape[1]
    return pl.pallas_call(
        _ffn_kernel,
        grid=(n // tm,),
        in_specs=[pl.BlockSpec((tm, d), lambda i: (i, 0)), _const_spec((1, d)),
                  _const_spec((d, f)), _const_spec((d, f)), _const_spec((f, d))],
        out_specs=pl.BlockSpec((tm, d), lambda i: (i, 0)),
        out_shape=jax.ShapeDtypeStruct((n, d), F32),
        compiler_params=_params(("parallel",)),
        name="ffn",
    )(h, gain, wg, wu, wd)


def _inproj_kernel(h_ref, g_ref, wa_ref, wb_ref, wc_ref, wd_ref, oa_ref, ob_ref, oc_ref, od_ref):
    u = _rms(h_ref[...], g_ref[...]).astype(BF16)
    for w_ref, o_ref in ((wa_ref, oa_ref), (wb_ref, ob_ref), (wc_ref, oc_ref), (wd_ref, od_ref)):
        o_ref[...] = jnp.dot(u, w_ref[...], preferred_element_type=F32).astype(o_ref.dtype)


def _inproj(h, gain, ws, tm):
    n, d = h.shape
    widths = [w.shape[1] for w in ws]
    dtypes = [F32, F32, F32, BF16]
    return pl.pallas_call(
        _inproj_kernel,
        grid=(n // tm,),
        in_specs=[pl.BlockSpec((tm, d), lambda i: (i, 0)), _const_spec((1, d))]
                 + [_const_spec((d, wd)) for wd in widths],
        out_specs=[pl.BlockSpec((tm, wd), lambda i: (i, 0)) for wd in widths],
        out_shape=[jax.ShapeDtypeStruct((n, wd), dt) for wd, dt in zip(widths, dtypes)],
        compiler_params=_params(("parallel",)),
        name="inproj",
    )(h, gain, *ws)


def _outproj_kernel(h_ref, ya_ref, yb_ref, yc_ref, yd_ref, w_ref, o_ref):
    acc = h_ref[...]
    for i, y_ref in enumerate((ya_ref, yb_ref, yc_ref, yd_ref)):
        acc = acc + jnp.dot(y_ref[...].astype(BF16), w_ref[i * GROUP_W:(i + 1) * GROUP_W, :],
                            preferred_element_type=F32)
    o_ref[...] = acc


def _outproj(h, ys, w, tm):
    n, d = h.shape
    return pl.pallas_call(
        _outproj_kernel,
        grid=(n // tm,),
        in_specs=[pl.BlockSpec((tm, d), lambda i: (i, 0))]
                 + [pl.BlockSpec((tm, GROUP_W), lambda i: (i, 0)) for _ in ys]
                 + [_const_spec(w.shape)],
        out_specs=pl.BlockSpec((tm, d), lambda i: (i, 0)),
        out_shape=jax.ShapeDtypeStruct((n, d), F32),
        compiler_params=_params(("parallel",)),
        name="outproj",
    )(h, *ys, w)


def _ple_kernel(h_ref, p_ref, g_ref, wg_ref, wp_ref, fg_ref, o_ref, *, final):
    x = h_ref[...]
    u = _rms(x, g_ref[...]).astype(BF16)
    gate = _sigmoid(jnp.dot(u, wg_ref[...], preferred_element_type=F32))
    e = jnp.dot(p_ref[...].astype(BF16), wp_ref[...], preferred_element_type=F32)
    y = x + e * gate
    if final:
        y = _rms(y, fg_ref[...])
    o_ref[...] = y


def _ple(h, p, gain, wg, wp, fgain, final, tm):
    n, d = h.shape
    pd = p.shape[1]
    return pl.pallas_call(
        functools.partial(_ple_kernel, final=final),
        grid=(n // tm,),
        in_specs=[pl.BlockSpec((tm, d), lambda i: (i, 0)), pl.BlockSpec((tm, pd), lambda i: (i, 0)),
                  _const_spec((1, d)), _const_spec((d, d)), _const_spec((pd, d)), _const_spec((1, d))],
        out_specs=pl.BlockSpec((tm, d), lambda i: (i, 0)),
        out_shape=jax.ShapeDtypeStruct((n, d), F32),
        compiler_params=_params(("parallel",)),
        name="ple",
    )(h, p, gain, wg, wp, fgain)


def _halo_map(rows_per_block):
    step = rows_per_block // HALO
    return lambda b, i: (b, jnp.maximum(i * step - 1, 0), 0)


def _mixer_call(kernel_fn, name, x, params):
    bsz, s, w = x.shape
    t = N_CHUNKS * CHUNK
    consts = _np_consts()
    return pl.pallas_call(
        kernel_fn,
        grid=(bsz, s // t),
        in_specs=[pl.BlockSpec((1, t, w), lambda b, i: (b, i, 0)),
                  pl.BlockSpec((1, HALO, w), _halo_map(t))]
                 + [_const_spec(c.shape) for c in consts]
                 + [_const_spec(p.shape) for p in params],
        out_specs=pl.BlockSpec((1, t, GROUP_W), lambda b, i: (b, i, 0)),
        out_shape=jax.ShapeDtypeStruct((bsz, s, GROUP_W), F32),
        scratch_shapes=[pltpu.VMEM((GROUP_W, GROUP_W), F32)],
        compiler_params=_params(("parallel", "arbitrary")),
        name=name,
    )(x, x, *consts, *params)


def _reset_state(st_ref):
    @pl.when(pl.program_id(1) == 0)
    def _():
        st_ref[...] = jnp.zeros_like(st_ref)


SSD_W = 1152


def _ssd_kernel(x_ref, halo_ref, bdf_ref, bdb_ref, cat_ref, sel_ref, exp_ref, pick_ref, catb_ref,
                cw_ref, cb_ref, dtb_ref, alog_ref, dsk_ref, nw_ref, o_ref, st_ref):
    _reset_state(st_ref)
    first = pl.program_id(1) == 0
    x = x_ref[0]
    z = x[:, 0:GROUP_W]
    xbc = _causal_conv(x[:, GROUP_W:4 * GROUP_W], halo_ref[0][:, GROUP_W:4 * GROUP_W], cw_ref, first)
    xbc = _silu(xbc + cb_ref[...])
    xs = xbc[:, 0:GROUP_W]
    bm = xbc[:, GROUP_W:2 * GROUP_W].astype(BF16)
    cm = xbc[:, 2 * GROUP_W:3 * GROUP_W].astype(BF16)
    dt_pad = _softplus(x[:, 4 * GROUP_W:] + dtb_ref[...])
    la_pad = dt_pad * (-jnp.exp(alog_ref[...]))
    dt_b = _mm_sel_r(dt_pad, exp_ref[0])
    la_b = _mm_sel_r(la_pad, exp_ref[0])
    cs = _mm_sel_l(sel_ref[SEL_LTRI], la_b)
    xc = xs * dt_b
    ecs = jnp.exp(cs)
    same = bdb_ref[BDB_SAME]
    grp_rows = bdb_ref[BDB_GRP_ROWS]
    grp_state = bdf_ref[BDF_GRP_STATE]
    chunks = [_chunk_rows(c) for c in range(N_CHUNKS)]
    neg = cat_ref[CAT_NEG]
    rforms = _row_form([cs[rs] for rs in chunks], bdb_ref, pick_ref)
    segs = [jnp.exp(cs[rs] - rf + neg) for rs, rf in zip(chunks, rforms)]
    scores = [(_dg(cm[rs], _rows(bm[rs], grp_rows), 1, 1) * sg).astype(BF16)
              for rs, sg in zip(chunks, segs)]
    xc_bf = xc.astype(BF16)
    ys = [_dg(sc, _rows(xc_bf[rs], same), 1, 0) for rs, sc in zip(chunks, scores)]
    lasts = [cs[rs][CHUNK - 1:CHUNK, :] for rs in chunks]
    upds = [_dg(bm[rs], (xc[rs] * jnp.exp(last - cs[rs])).astype(BF16), 0, 0) * grp_state
            for rs, last in zip(chunks, lasts)]
    st = st_ref[...]
    for c, rs in enumerate(chunks):
        ys[c] = ys[c] + _dg(cm[rs], st.astype(BF16), 1, 0) * ecs[rs]
        st = st * jnp.exp(lasts[c]) + upds[c]
    st_ref[...] = st
    y = (jnp.concatenate(ys, axis=0) + dsk_ref[...] * xs) * _silu(z)
    ms = _mm(y * y, sel_ref[SEL_ONES_G]) * (1.0 / SSM_STATE)
    o_ref[0] = y * lax.rsqrt(ms + NORM_EPS) * nw_ref[...]


GDN_W = 1152


def _gdn_kernel(x_ref, halo_ref, bdf_ref, bdb_ref, cat_ref, sel_ref, exp_ref, pick_ref, catb_ref,
                cw_ref, alog_ref, dtb_ref, nw_ref, o_ref, st_ref):
    _reset_state(st_ref)
    first = pl.program_id(1) == 0
    ones_h = sel_ref[SEL_ONES_H]
    x = x_ref[0]
    qkv = _silu(_causal_conv(x[:, 0:3 * GROUP_W], halo_ref[0][:, 0:3 * GROUP_W], cw_ref, first))
    z = x[:, 3 * GROUP_W:4 * GROUP_W]
    ba = x[:, 4 * GROUP_W:]
    q = qkv[:, 0:GROUP_W]
    k = qkv[:, GROUP_W:2 * GROUP_W]
    v = qkv[:, 2 * GROUP_W:3 * GROUP_W]
    q = q * lax.rsqrt(_mm(q * q, ones_h) + L2_EPS) * (HEAD_DIM ** -0.5)
    k = k * lax.rsqrt(_mm(k * k, ones_h) + L2_EPS)
    beta_pad = _sigmoid(ba)
    g_pad = -jnp.exp(alog_ref[...]) * _softplus(ba + dtb_ref[...])
    beta_b = _mm_sel_r2(beta_pad, exp_ref[0])
    g_b = _mm_sel_r(g_pad, exp_ref[1])
    gc = _mm_sel_l(sel_ref[SEL_LTRI], g_b)
    eg = jnp.exp(gc)
    kb = k * beta_b
    q_bf = q.astype(BF16)
    k_bf = k.astype(BF16)
    kb_bf = kb.astype(BF16)
    vb_bf = (v * beta_b).astype(BF16)
    kbg_bf = (kb * eg).astype(BF16)
    qg_bf = (q * eg).astype(BF16)
    same = bdb_ref[BDB_SAME]
    chunks = [_chunk_rows(c) for c in range(N_CHUNKS)]
    neg = cat_ref[CAT_NEG]
    strict = cat_ref[CAT_STRICT]
    rforms = _row_form([gc[rs] for rs in chunks], bdb_ref, pick_ref)
    decays = [jnp.exp(gc[rs] - rf + neg) for rs, rf in zip(chunks, rforms)]
    k_rows = [_rows(k_bf[rs], same) for rs in chunks]
    a_cats = [(_dg(kb_bf[rs], kr, 1, 1) * dc * strict).astype(BF16)
              for rs, kr, dc in zip(chunks, k_rows, decays)]
    qks = [(_dg(q_bf[rs], kr, 1, 1) * dc).astype(BF16) for rs, kr, dc in zip(chunks, k_rows, decays)]
    t_cats = _inv_unit_lower(a_cats, catb_ref, same)
    uws = [_dg(t_cat, jnp.concatenate([_rows(vb_bf[rs], same), _rows(kbg_bf[rs], same)], axis=1), 1, 0)
           for rs, t_cat in zip(chunks, t_cats)]
    lasts = [gc[rs][CHUNK - 1:CHUNK, :] for rs in chunks]
    k_decs = [(k[rs] * jnp.exp(last - gc[rs])).astype(BF16) for rs, last in zip(chunks, lasts)]
    same_f = bdf_ref[BDF_SAME]
    us = [uw[:, 0:GROUP_W] for uw in uws]
    ws_bf = [uw[:, GROUP_W:].astype(BF16) for uw in uws]
    m_st = [(-same_f * _dg(kd, w, 0, 0)).astype(BF16) for kd, w in zip(k_decs, ws_bf)]
    c_st = [same_f * _dg(kd, u.astype(BF16), 0, 0) for kd, u in zip(k_decs, us)]
    st = st_ref[...]
    sts = []
    for c in range(N_CHUNKS):
        st_bf = st.astype(BF16)
        sts.append(st_bf)
        st = st * jnp.exp(lasts[c]) + _dg(m_st[c], st_bf, 1, 0) + c_st[c]
    st_ref[...] = st
    v_news = [(u - _dg(w, s_bf, 1, 0)).astype(BF16) for u, w, s_bf in zip(us, ws_bf, sts)]
    os_ = [_dg(qg_bf[rs], s_bf, 1, 0) + _dg(qk, _rows(vn, same), 1, 0)
           for rs, s_bf, qk, vn in zip(chunks, sts, qks, v_news)]
    o = jnp.concatenate(os_, axis=0)
    ms = _mm(o * o, ones_h) * (1.0 / HEAD_DIM)
    o_ref[0] = o * lax.rsqrt(ms + NORM_EPS) * nw_ref[...] * _silu(z)


RWKV_W = 896


def _rwkv_kernel(x_ref, halo_ref, bdf_ref, bdb_ref, cat_ref, sel_ref, exp_ref, pick_ref, catb_ref,
                 mu_ref, w0_ref, wup_ref, a0_ref, aup_ref, gup_ref, kk_ref, ka_ref,
                 rk_ref, lnw_ref, lnb_ref, o_ref, st_ref):
    _reset_state(st_ref)
    first = pl.program_id(1) == 0
    ones_h = sel_ref[SEL_ONES_H]
    x = x_ref[0]
    halo = jnp.where(first, 0.0, halo_ref[0])
    shifted = pltpu.roll(jnp.concatenate([halo, x], axis=0), 1, axis=0)[HALO:]
    x = x + mu_ref[...] * (shifted - x)
    r = x[:, 0:GROUP_W]
    k = x[:, GROUP_W:2 * GROUP_W]
    v = x[:, 2 * GROUP_W:3 * GROUP_W]
    lora = x[:, 3 * GROUP_W:]
    w_log = -_softplus(-(w0_ref[...] + _mm(jnp.tanh(lora), wup_ref[...]))) - RWKV_DECAY_OFFSET
    lw = -jnp.exp(w_log)
    a_gate = _sigmoid(a0_ref[...] + _mm(lora, aup_ref[...]))
    g = _mm(_sigmoid(lora), gup_ref[...])
    kk = k * kk_ref[...]
    kk = kk * lax.rsqrt(_mm(kk * kk, ones_h) + L2_EPS)
    k = k * (1.0 + (a_gate - 1.0) * ka_ref[...])
    b_v = kk * a_gate
    cl = _mm_sel_l(sel_ref[SEL_LTRI], lw)
    e_in = jnp.exp(cl)
    e_inv = jnp.exp(-cl)
    at_bf = (-kk * jnp.exp(cl - lw)).astype(BF16)
    rt_bf = (r * e_in).astype(BF16)
    bt_bf = (b_v * e_inv).astype(BF16)
    kt_bf = (k * e_inv).astype(BF16)
    v_bf = v.astype(BF16)
    same = bdb_ref[BDB_SAME]
    strict = cat_ref[CAT_STRICT]
    incl = cat_ref[CAT_INCL]
    chunks = [_chunk_rows(c) for c in range(N_CHUNKS)]
    zero = jnp.zeros((), F32)
    b_rows = [_rows(bt_bf[rs], same) for rs in chunks]
    k_rows = [_rows(kt_bf[rs], same) for rs in chunks]
    v_rows = [_rows(v_bf[rs], same) for rs in chunks]
    n_cats = [jnp.where(strict > 0, -_dg(at_bf[rs], br, 1, 1), zero).astype(BF16)
              for rs, br in zip(chunks, b_rows)]
    aks = [jnp.where(strict > 0, _dg(at_bf[rs], kr, 1, 1), zero).astype(BF16)
           for rs, kr in zip(chunks, k_rows)]
    rbs = [jnp.where(incl > 0, _dg(rt_bf[rs], br, 1, 1), zero).astype(BF16)
           for rs, br in zip(chunks, b_rows)]
    rks = [jnp.where(incl > 0, _dg(rt_bf[rs], kr, 1, 1), zero).astype(BF16)
           for rs, kr in zip(chunks, k_rows)]
    t_cats = _inv_unit_lower(n_cats, catb_ref, same)
    tas = [_dg(t_cat, _rows(at_bf[rs], same), 1, 0).astype(BF16) for rs, t_cat in zip(chunks, t_cats)]
    tvs = [_dg(t_cat, _rows(_dg(ak, vr, 1, 0).astype(BF16), same), 1, 0)
           for t_cat, ak, vr in zip(t_cats, aks, v_rows)]
    y_vs = [_dg(rk, vr, 1, 0) for rk, vr in zip(rks, v_rows)]
    lasts = [cl[rs][CHUNK - 1:CHUNK, :] for rs in chunks]
    to_ends = [jnp.exp(last - cl[rs]) for rs, last in zip(chunks, lasts)]
    same_f = bdf_ref[BDF_SAME]
    b_ends = [(b_v[rs] * te).astype(BF16) for rs, te in zip(chunks, to_ends)]
    k_ends = [(k[rs] * te).astype(BF16) for rs, te in zip(chunks, to_ends)]
    m_st = [(same_f * _dg(ta, be, 0, 0)).astype(BF16) for ta, be in zip(tas, b_ends)]
    c_st = [same_f * _dg(jnp.concatenate([tv.astype(BF16), v_bf[rs]], axis=0),
                         jnp.concatenate([be, ke], axis=0), 0, 0)
            for tv, rs, be, ke in zip(tvs, chunks, b_ends, k_ends)]
    st = st_ref[...]
    sts = []
    for c in range(N_CHUNKS):
        st_bf = st.astype(BF16)
        sts.append(st_bf)
        st = st * jnp.exp(lasts[c]) + _dg(st_bf, m_st[c], 1, 0) + c_st[c]
    st_ref[...] = st
    us = [(_dg(ta, s_bf, 1, 1) + tv).astype(BF16) for ta, tv, s_bf in zip(tas, tvs, sts)]
    ys = [_dg(rt_bf[rs], s_bf, 1, 1) + _dg(rb, _rows(u, same), 1, 0) + yv
          for rs, s_bf, rb, u, yv in zip(chunks, sts, rbs, us, y_vs)]
    y = jnp.concatenate(ys, axis=0)
    inv_d = 1.0 / HEAD_DIM
    mean = _mm(y, ones_h) * inv_d
    yc = y - mean
    var = _mm(yc * yc, ones_h) * inv_d
    yn = yc * lax.rsqrt(var + RWKV_GN_EPS) * lnw_ref[...] + lnb_ref[...]
    bonus = _mm(r * k * rk_ref[...], ones_h) * v
    o_ref[0] = (yn + bonus) * g


SB_BLOCK = 256
SB_NEGLIGIBLE = 128.0


def _neg_abs(x):
    sign = jnp.uint32(0x80000000)
    return lax.bitcast_convert_type(lax.bitcast_convert_type(x, jnp.uint32) | sign, F32)


def _sb_kernel(q_ref, k_ref, v_ref, o_ref, acc_ref, run_ref, zz_ref, w_ref):
    qi = pl.program_id(1)
    t = SB_BLOCK
    heads = range(N_HEADS)
    scale = HEAD_DIM ** -0.5
    assert scale == 0.125
    lane_head = _iota((t, GROUP_W), 1) // HEAD_DIM
    q = q_ref[0] * jnp.asarray(scale, q_ref.dtype)
    zero = jnp.zeros((), q.dtype)
    q_h = [jnp.where(lane_head == h, q, zero) for h in heads]
    tri = (_iota((t, t), 0) >= _iota((t, t), 1)).astype(BF16)
    upper2 = jnp.concatenate([tri, tri], axis=0)
    causal = _iota((t, t), 1) < _iota((t, t), 0)
    acc_ref[...] = jnp.zeros_like(acc_ref)
    run_ref[...] = jnp.zeros_like(run_ref)

    def scores(j):
        kj = k_ref[0, pl.ds(pl.multiple_of(j * t, t), t), :]
        return [_dg(q_h[h], kj, 1, 1) for h in heads]

    def weights(zz, masked):
        parts, sums = [], []
        for h in heads:
            sp = jnp.maximum(zz[h], 0.0) + jnp.log(1.0 + jnp.exp(_neg_abs(zz[h])))
            if masked:
                sp = jnp.where(causal, sp, 0.0)
            hi, lo = _split2(sp)
            parts.append(jnp.concatenate([hi, lo], axis=1))
            sums.append(jnp.sum(sp, axis=1, keepdims=True))
        later = [_dg(p, upper2, 1, 0) for p in parts]
        wgts = []
        for h in heads:
            run = run_ref[h]
            wgt = jnp.exp(zz[h] - later[h] - jnp.concatenate([run] * (t // LANE), axis=1))
            if masked:
                wgt = jnp.where(causal, wgt, 0.0)
            wgts.append(wgt.astype(BF16))
            run_ref[h] = run + sums[h]
        return jnp.concatenate(wgts, axis=1)

    def accumulate(w_cat, j):
        vj = v_ref[0, pl.ds(pl.multiple_of(j * t, t), t), :]
        v_cat = jnp.concatenate([jnp.where(lane_head == h, vj, zero) for h in heads], axis=0)
        acc_ref[...] += _dg(w_cat, v_cat, 1, 0)

    def stash(zz, w_cat):
        for h in heads:
            zz_ref[h] = zz[h]
        w_ref[...] = w_cat

    zz_next = scores(jnp.maximum(qi - 1, 0))
    stash(zz_next, weights(scores(qi), True))

    def run_min():
        r = run_ref[0]
        for h in range(1, N_HEADS):
            r = jnp.minimum(r, run_ref[h])
        return jnp.min(r)

    def cond(carry):
        i, rmin = carry
        return jnp.logical_and(i <= qi, rmin < SB_NEGLIGIBLE)

    def body(carry):
        i, _ = carry
        j = qi - i
        accumulate(w_ref[...], j + 1)
        zz_next = scores(jnp.maximum(j - 1, 0))
        stash(zz_next, weights([zz_ref[h] for h in heads], False))
        return i + 1, run_min()

    i_end, _ = lax.while_loop(cond, body, (jnp.int32(1), run_min()))
    accumulate(w_ref[...], qi - (i_end - 1))
    o_ref[0] = acc_ref[...]


def _sb(pd):
    bsz, s, _ = pd.shape
    t = SB_BLOCK
    return pl.pallas_call(
        _sb_kernel,
        grid=(bsz, s // t),
        in_specs=[pl.BlockSpec((1, t, GROUP_W), lambda b, i: (b, i, 0)),
                  pl.BlockSpec((1, s, GROUP_W), lambda b, i: (b, 0, 1)),
                  pl.BlockSpec((1, s, GROUP_W), lambda b, i: (b, 0, 2))],
        out_specs=pl.BlockSpec((1, t, GROUP_W), lambda b, i: (b, i, 0)),
        out_shape=jax.ShapeDtypeStruct((bsz, s, GROUP_W), F32),
        scratch_shapes=[pltpu.VMEM((t, GROUP_W), F32), pltpu.VMEM((N_HEADS, t, LANE), F32),
                        pltpu.VMEM((N_HEADS, t, t), F32), pltpu.VMEM((t, N_HEADS * t), BF16)],
        compiler_params=_params(("parallel", "arbitrary")),
        name="stickbreak",
    )(pd, pd, pd)


def _row(v, width=None):
    v = v.reshape(1, -1).astype(F32)
    if width is not None and v.shape[1] < width:
        v = jnp.pad(v, ((0, 0), (0, width - v.shape[1])))
    return v


def _pad_cols(w, width):
    return jnp.pad(w, ((0, 0), (0, width - w.shape[1])))


def _pad_rows_at(w, offset, total):
    return jnp.pad(w, ((offset, total - offset - w.shape[0]), (0, 0)))


def _mixer(h2, bsz, s, gain, w_in, ssm_conv_w, ssm_conv_b, ssm_dt_bias, ssm_a_log, ssm_d, ssm_norm,
           rwkv_mu, rwkv_w0, rwkv_w_up, rwkv_a0, rwkv_a_up, rwkv_g_up, rwkv_k_k, rwkv_k_a, rwkv_r_k,
           rwkv_ln_w, rwkv_ln_b, gdn_conv_w, gdn_a_log, gdn_dt_bias, gdn_norm, w_out, tm):
    gw = GROUP_W
    ssm_in = 4 * gw + N_HEADS
    c0 = ssm_in
    c1 = c0 + RWKV_W
    c2 = c1 + 4 * gw + 2 * N_HEADS
    wa = _pad_cols(w_in[:, :c0], SSD_W).astype(BF16)
    wb = w_in[:, c0:c1].astype(BF16)
    wc = _pad_cols(w_in[:, c1:c2], GDN_W).astype(BF16)
    wd = w_in[:, c2:].astype(BF16)
    pa, pb, pc, pd = _inproj(h2, _row(gain), [wa, wb, wc, wd], tm)

    ya = _mixer_call(_ssd_kernel, "ssd", pa.reshape(bsz, s, SSD_W),
                     [ssm_conv_w.astype(F32), _row(ssm_conv_b), _row(ssm_dt_bias, LANE),
                      _row(ssm_a_log, LANE), _row(jnp.repeat(ssm_d, HEAD_DIM)), _row(ssm_norm)])

    lora_w = RWKV_W - 3 * gw
    yb = _mixer_call(_rwkv_kernel, "rwkv", pb.reshape(bsz, s, RWKV_W),
                     [_row(rwkv_mu), _row(rwkv_w0),
                      _pad_rows_at(rwkv_w_up, 0, lora_w).astype(BF16), _row(rwkv_a0),
                      _pad_rows_at(rwkv_a_up, rwkv_w_up.shape[0], lora_w).astype(BF16),
                      _pad_rows_at(rwkv_g_up, rwkv_w_up.shape[0] + rwkv_a_up.shape[0], lora_w).astype(BF16),
                      _row(rwkv_k_k), _row(rwkv_k_a), _row(rwkv_r_k), _row(rwkv_ln_w), _row(rwkv_ln_b)])

    pad4 = lambda t: jnp.pad(t.reshape(1, -1).astype(F32), ((0, 0), (N_HEADS, LANE - 2 * N_HEADS)))
    yc = _mixer_call(_gdn_kernel, "gdn", pc.reshape(bsz, s, GDN_W),
                     [gdn_conv_w.astype(F32), pad4(gdn_a_log), pad4(gdn_dt_bias),
                      _row(jnp.tile(gdn_norm, N_HEADS))])

    yd = _sb(pd.reshape(bsz, s, 3 * gw))

    n = bsz * s
    return _outproj(h2, [ya.reshape(n, gw), yb.reshape(n, gw), yc.reshape(n, gw), yd.reshape(n, gw)],
                    w_out.astype(BF16), tm)


def kernel(x, p, ffn1_norm, ffn1_w_gate, ffn1_w_up, ffn1_w_down, mix_norm, w_in, ssm_conv_w, ssm_conv_b, ssm_dt_bias, ssm_a_log, ssm_d, ssm_norm, rwkv_mu, rwkv_w0, rwkv_w_up, rwkv_a0, rwkv_a_up, rwkv_g_up, rwkv_k_k, rwkv_k_a, rwkv_r_k, rwkv_ln_w, rwkv_ln_b, gdn_conv_w, gdn_a_log, gdn_dt_bias, gdn_norm, w_out, ffn2_norm, ffn2_w_gate, ffn2_w_up, ffn2_w_down, ple_norm, ple_w_gate, ple_w_proj, final_norm):
    bsz, s, d = x.shape
    depth = p.shape[0]
    n = bsz * s
    tm = min(512, n)
    h = x.reshape(n, d)
    for i in range(depth):
        h = _ffn(h, _row(ffn1_norm[i]), ffn1_w_gate[i].astype(BF16), ffn1_w_up[i].astype(BF16),
                 ffn1_w_down[i].astype(BF16), tm)
        h = _mixer(h, bsz, s, mix_norm[i], w_in[i], ssm_conv_w[i], ssm_conv_b[i], ssm_dt_bias[i],
                   ssm_a_log[i], ssm_d[i], ssm_norm[i], rwkv_mu[i], rwkv_w0[i], rwkv_w_up[i],
                   rwkv_a0[i], rwkv_a_up[i], rwkv_g_up[i], rwkv_k_k[i], rwkv_k_a[i], rwkv_r_k[i],
                   rwkv_ln_w[i], rwkv_ln_b[i], gdn_conv_w[i], gdn_a_log[i], gdn_dt_bias[i],
                   gdn_norm[i], w_out[i], tm)
        h = _ffn(h, _row(ffn2_norm[i]), ffn2_w_gate[i].astype(BF16), ffn2_w_up[i].astype(BF16),
                 ffn2_w_down[i].astype(BF16), tm)
        h = _ple(h, p[i].reshape(n, -1), _row(ple_norm[i]), ple_w_gate[i].astype(BF16),
                 ple_w_proj[i].astype(BF16), _row(final_norm), i == depth - 1, tm)
    return h.reshape(bsz, s, d)
```

```python
import functools

import numpy as np
import jax
import jax.numpy as jnp
from jax import lax
from jax.experimental import pallas as pl
from jax.experimental.pallas import tpu as pltpu

F32 = jnp.float32
BF16 = jnp.bfloat16

HEAD_DIM = 64
N_HEADS = 4
GROUP_W = HEAD_DIM * N_HEADS
CHUNK = 64
N_CHUNKS = 4
CONV_K = 4
SSM_STATE = 128
NORM_EPS = 1e-6
L2_EPS = 1e-6
RWKV_GN_EPS = 64e-5
RWKV_DECAY_OFFSET = 0.5
HALO = 8
LANE = 128
VMEM_LIMIT = 56 * 1024 * 1024


def _dg(a, b, ca, cb):
    return lax.dot_general(a, b, (((ca,), (cb,)), ((), ())), preferred_element_type=F32)


_DIMS = {"nn": (1, 0), "nt": (1, 1), "tn": (0, 0)}


def _mm(a, b, kind="nn"):
    ca, cb = _DIMS[kind]
    return _dg(a.astype(BF16), b.astype(BF16), ca, cb)


def _split2(x):
    hi = x.astype(BF16)
    lo = (x - hi.astype(F32)).astype(BF16)
    return hi, lo


def _split3(x):
    hi = x.astype(BF16)
    r = x - hi.astype(F32)
    mid = r.astype(BF16)
    lo = (r - mid.astype(F32)).astype(BF16)
    return hi, mid, lo


def _mm_sel_l(sel, b, kind="nn"):
    ca, cb = _DIMS[kind]
    h, m, l = _split3(b)
    return _dg(sel, h, ca, cb) + (_dg(sel, m, ca, cb) + _dg(sel, l, ca, cb))


def _mm_sel_r(a, sel, kind="nn"):
    ca, cb = _DIMS[kind]
    h, m, l = _split3(a)
    return _dg(h, sel, ca, cb) + (_dg(m, sel, ca, cb) + _dg(l, sel, ca, cb))


def _mm_sel_r2(a, sel):
    h, l = _split2(a)
    return _dg(h, sel, 1, 0) + _dg(l, sel, 1, 0)


def _sigmoid(x):
    return 1.0 / (1.0 + jnp.exp(-x))


def _silu(x):
    return x * _sigmoid(x)


def _softplus(x):
    return jnp.maximum(x, 0.0) + jnp.log1p(jnp.exp(-jnp.abs(x)))


def _iota(shape, dim):
    return lax.broadcasted_iota(jnp.int32, shape, dim)


def _tile_rows(x, n):
    return jnp.concatenate([x] * n, axis=0)


def _rms(x, gain_row):
    ms = jnp.mean(x * x, axis=-1, keepdims=True)
    return x * lax.rsqrt(ms + NORM_EPS) * gain_row


BDF_SAME, BDF_GRP_STATE = range(2)
BDB_SAME, BDB_GRP_ROWS = range(2)
CAT_NEG, CAT_STRICT, CAT_INCL = range(3)
CATB_EYE, CATB_M8, CATB_OFF8, CATB_OFF16, CATB_OFF32 = range(5)
SEL_ONES_H, SEL_ONES_G, SEL_LTRI = range(3)


def _np_consts():
    r = np.arange(GROUP_W)[:, None]
    c = np.arange(GROUP_W)[None, :]
    same = (r // HEAD_DIM) == (c // HEAD_DIM)

    bdf = np.stack([same, (r // SSM_STATE) == ((c // HEAD_DIM) // 2)]).astype(np.float32)
    bdb = np.stack([same, ((r // HEAD_DIM) // 2) == (c // SSM_STATE)]).astype(np.float32)
    l = np.arange(CHUNK)[:, None]
    s = np.arange(GROUP_W)[None, :] % CHUNK
    cat = np.stack([np.where(l >= s, 0.0, -np.inf), l > s, l >= s]).astype(np.float32)

    def off(b):
        return ((l // (2 * b)) == (s // (2 * b))) & (((l // b) % 2) == 1) & (((s // b) % 2) == 0)

    catb = np.stack([l == s, (l // 8) == (s // 8), off(8), off(16), off(32)]).astype(np.float32)
    t = np.arange(N_CHUNKS * CHUNK)
    ltri = (t[:, None] >= t[None, :]) & ((t[:, None] // CHUNK) == (t[None, :] // CHUNK))
    sel = np.stack([same, (r // SSM_STATE) == (c // SSM_STATE), ltri]).astype(np.float32)
    er = np.arange(LANE)[:, None]
    ec = np.arange(GROUP_W)[None, :] // HEAD_DIM
    expand = np.stack([er == ec, er == ec + N_HEADS]).astype(np.float32)
    pick = np.broadcast_to(np.arange(GROUP_W)[None, :] % HEAD_DIM == 0, (CHUNK, GROUP_W)).astype(np.float32)
    return (jnp.asarray(bdf), jnp.asarray(bdb, BF16), jnp.asarray(cat), jnp.asarray(sel, BF16),
            jnp.asarray(expand, BF16), jnp.asarray(pick, BF16), jnp.asarray(catb, BF16))


def _rows(x_bf, mask_bf):
    return _tile_rows(x_bf, N_HEADS) * mask_bf


def _row_form(colvals, bdb_ref, pick_ref):
    same = bdb_ref[BDB_SAME]
    pick = pick_ref[...]
    parts = [_split3(x) for x in colvals]
    prods = [[_dg(pick, _rows(p, same), 1, 1) for p in ps] for ps in parts]
    return [h + (m + l) for h, m, l in prods]


def _inv_unit_lower(n_cats_bf, catb_ref, same):
    def mul(a_cat_bf, b_cat_bf):
        return _dg(a_cat_bf, _rows(b_cat_bf, same), 1, 0)

    eye = catb_ref[CATB_EYE]
    m8 = catb_ref[CATB_M8]
    d = [n * m8 for n in n_cats_bf]
    d2 = [mul(x, x).astype(BF16) for x in d]
    d4 = [mul(x, x).astype(BF16) for x in d2]
    t = [mul(eye - x, eye + y) for x, y in zip(d, d2)]
    t = [mul(x.astype(BF16), eye + y) for x, y in zip(t, d4)]
    for idx in (CATB_OFF8, CATB_OFF16, CATB_OFF32):
        off = catb_ref[idx]
        tb = [x.astype(BF16) for x in t]
        to = [mul(x, n * off).astype(BF16) for x, n in zip(tb, n_cats_bf)]
        t = [x - mul(y, z) for x, y, z in zip(t, to, tb)]
    return [x.astype(BF16) for x in t]


def _causal_conv(cur, halo, w_ref, first):
    halo = jnp.where(first, 0.0, halo)
    ext = jnp.concatenate([halo, cur], axis=0)
    acc = cur * w_ref[CONV_K - 1:CONV_K, :]
    for j in range(CONV_K - 1):
        sh = pltpu.roll(ext, CONV_K - 1 - j, axis=0)[HALO:]
        acc = acc + sh * w_ref[j:j + 1, :]
    return acc


def _chunk_rows(c):
    return slice(c * CHUNK, (c + 1) * CHUNK)


def _ffn_kernel(h_ref, g_ref, wg_ref, wu_ref, wd_ref, o_ref):
    x = h_ref[...]
    u = _rms(x, g_ref[...]).astype(BF16)
    a = jnp.dot(u, wg_ref[...], preferred_element_type=F32)
    b = jnp.dot(u, wu_ref[...], preferred_element_type=F32)
    act = (_silu(a) * b).astype(BF16)
    y = jnp.dot(act, wd_ref[...], preferred_element_type=F32)
    o_ref[...] = x + 0.5 * y


def _const_spec(shape):
    nd = len(shape)
    return pl.BlockSpec(shape, lambda *_: (0,) * nd, pipeline_mode=pl.Buffered(1))


def _params(sem):
    return pltpu.CompilerParams(dimension_semantics=sem, vmem_limit_bytes=VMEM_LIMIT)


def _ffn(h, gain, wg, wu, wd, tm):
    n, d = h.shape
    f = wg.shape[1]
    return pl.pallas_call(
        _ffn_kernel,
        grid=(n // tm,),
        in_specs=[pl.BlockSpec((tm, d), lambda i: (i, 0)), _const_spec((1, d)),
                  _const_spec((d, f)), _const_spec((d, f)), _const_spec((f, d))],
        out_specs=pl.BlockSpec((tm, d), lambda i: (i, 0)),
        out_shape=jax.ShapeDtypeStruct((n, d), F32),
        compiler_params=_params(("parallel",)),
        name="ffn",
    )(h, gain, wg, wu, wd)


def _inproj_kernel(h_ref, g_ref, wa_ref, wb_ref, wc_ref, wd_ref, oa_ref, ob_ref, oc_ref, od_ref):
    u = _rms(h_ref[...], g_ref[...]).astype(BF16)
    for w_ref, o_ref in ((wa_ref, oa_ref), (wb_ref, ob_ref), (wc_ref, oc_ref), (wd_ref, od_ref)):
        o_ref[...] = jnp.dot(u, w_ref[...], preferred_element_type=F32).astype(o_ref.dtype)


def _inproj(h, gain, ws, tm):
    n, d = h.shape
    widths = [w.shape[1] for w in ws]
    dtypes = [F32, F32, F32, BF16]
    return pl.pallas_call(
        _inproj_kernel,
        grid=(n // tm,),
        in_specs=[pl.BlockSpec((tm, d), lambda i: (i, 0)), _const_spec((1, d))]
                 + [_const_spec((d, wd)) for wd in widths],
        out_specs=[pl.BlockSpec((tm, wd), lambda i: (i, 0)) for wd in widths],
        out_shape=[jax.ShapeDtypeStruct((n, wd), dt) for wd, dt in zip(widths, dtypes)],
        compiler_params=_params(("parallel",)),
        name="inproj",
    )(h, gain, *ws)


def _post_kernel(h_ref, ya_ref, yb_ref, yc_ref, yd_ref, p_ref, wo_ref, g2_ref, wg_ref, wu_ref, wd_ref,
                 gp_ref, wpg_ref, wpp_ref, fg_ref, o_ref, *, final):
    x = h_ref[...]
    for i, y_ref in enumerate((ya_ref, yb_ref, yc_ref, yd_ref)):
        x = x + jnp.dot(y_ref[...], wo_ref[i * GROUP_W:(i + 1) * GROUP_W, :], preferred_element_type=F32)
    u = _rms(x, g2_ref[...]).astype(BF16)
    a = jnp.dot(u, wg_ref[...], preferred_element_type=F32)
    b = jnp.dot(u, wu_ref[...], preferred_element_type=F32)
    act = (_silu(a) * b).astype(BF16)
    x = x + 0.5 * jnp.dot(act, wd_ref[...], preferred_element_type=F32)
    u = _rms(x, gp_ref[...]).astype(BF16)
    gate = _sigmoid(jnp.dot(u, wpg_ref[...], preferred_element_type=F32))
    e = jnp.dot(p_ref[...].astype(BF16), wpp_ref[...], preferred_element_type=F32)
    y = x + e * gate
    if final:
        y = _rms(y, fg_ref[...])
    o_ref[...] = y


def _post(h, ys, p3, layer, consts, final, tm):
    n, d = h.shape
    pd = p3.shape[2]
    return pl.pallas_call(
        functools.partial(_post_kernel, final=final),
        grid=(n // tm,),
        in_specs=[pl.BlockSpec((tm, d), lambda i: (i, 0))]
                 + [pl.BlockSpec((tm, GROUP_W), lambda i: (i, 0)) for _ in ys]
                 + [pl.BlockSpec((None, tm, pd), lambda i: (layer, i, 0))]
                 + [_const_spec(c.shape) for c in consts],
        out_specs=pl.BlockSpec((tm, d), lambda i: (i, 0)),
        out_shape=jax.ShapeDtypeStruct((n, d), F32),
        compiler_params=_params(("parallel",)),
        name="post",
    )(h, *ys, p3, *consts)


def _halo_map(rows_per_block):
    step = rows_per_block // HALO
    return lambda b, i: (b, jnp.maximum(i * step - 1, 0), 0)


def _mixer_call(kernel_fn, name, x, params):
    bsz, s, w = x.shape
    t = N_CHUNKS * CHUNK
    consts = _np_consts()
    return pl.pallas_call(
        kernel_fn,
        grid=(bsz, s // t),
        in_specs=[pl.BlockSpec((1, t, w), lambda b, i: (b, i, 0)),
                  pl.BlockSpec((1, HALO, w), _halo_map(t))]
                 + [_const_spec(c.shape) for c in consts]
                 + [_const_spec(p.shape) for p in params],
        out_specs=pl.BlockSpec((1, t, GROUP_W), lambda b, i: (b, i, 0)),
        out_shape=jax.ShapeDtypeStruct((bsz, s, GROUP_W), BF16),
        scratch_shapes=[pltpu.VMEM((GROUP_W, GROUP_W), F32)],
        compiler_params=_params(("parallel", "arbitrary")),
        name=name,
    )(x, x, *consts, *params)


def _reset_state(st_ref):
    @pl.when(pl.program_id(1) == 0)
    def _():
        st_ref[...] = jnp.zeros_like(st_ref)


SSD_W = 1152


def _ssd_kernel(x_ref, halo_ref, bdf_ref, bdb_ref, cat_ref, sel_ref, exp_ref, pick_ref, catb_ref,
                cw_ref, cb_ref, dtb_ref, alog_ref, dsk_ref, nw_ref, o_ref, st_ref):
    _reset_state(st_ref)
    first = pl.program_id(1) == 0
    x = x_ref[0]
    z = x[:, 0:GROUP_W]
    xbc = _causal_conv(x[:, GROUP_W:4 * GROUP_W], halo_ref[0][:, GROUP_W:4 * GROUP_W], cw_ref, first)
    xbc = _silu(xbc + cb_ref[...])
    xs = xbc[:, 0:GROUP_W]
    bm = xbc[:, GROUP_W:2 * GROUP_W].astype(BF16)
    cm = xbc[:, 2 * GROUP_W:3 * GROUP_W].astype(BF16)
    dt_pad = _softplus(x[:, 4 * GROUP_W:] + dtb_ref[...])
    la_pad = dt_pad * (-jnp.exp(alog_ref[...]))
    dt_b = _mm_sel_r(dt_pad, exp_ref[0])
    la_b = _mm_sel_r(la_pad, exp_ref[0])
    cs = _mm_sel_l(sel_ref[SEL_LTRI], la_b)
    xc = xs * dt_b
    ecs = jnp.exp(cs)
    same = bdb_ref[BDB_SAME]
    grp_rows = bdb_ref[BDB_GRP_ROWS]
    grp_state = bdf_ref[BDF_GRP_STATE]
    chunks = [_chunk_rows(c) for c in range(N_CHUNKS)]
    neg = cat_ref[CAT_NEG]
    rforms = _row_form([cs[rs] for rs in chunks], bdb_ref, pick_ref)
    segs = [jnp.exp(cs[rs] - rf + neg) for rs, rf in zip(chunks, rforms)]
    scores = [(_dg(cm[rs], _rows(bm[rs], grp_rows), 1, 1) * sg).astype(BF16)
              for rs, sg in zip(chunks, segs)]
    xc_bf = xc.astype(BF16)
    ys = [_dg(sc, _rows(xc_bf[rs], same), 1, 0) for rs, sc in zip(chunks, scores)]
    lasts = [cs[rs][CHUNK - 1:CHUNK, :] for rs in chunks]
    upds = [_dg(bm[rs], (xc[rs] * jnp.exp(last - cs[rs])).astype(BF16), 0, 0) * grp_state
            for rs, last in zip(chunks, lasts)]
    st = st_ref[...]
    for c, rs in enumerate(chunks):
        ys[c] = ys[c] + _dg(cm[rs], st.astype(BF16), 1, 0) * ecs[rs]
        st = st * jnp.exp(lasts[c]) + upds[c]
    st_ref[...] = st
    y = (jnp.concatenate(ys, axis=0) + dsk_ref[...] * xs) * _silu(z)
    ms = _mm(y * y, sel_ref[SEL_ONES_G]) * (1.0 / SSM_STATE)
    o_ref[0] = (y * lax.rsqrt(ms + NORM_EPS) * nw_ref[...]).astype(o_ref.dtype)


GDN_W = 1152


def _gdn_kernel(x_ref, halo_ref, bdf_ref, bdb_ref, cat_ref, sel_ref, exp_ref, pick_ref, catb_ref,
                cw_ref, alog_ref, dtb_ref, nw_ref, o_ref, st_ref):
    _reset_state(st_ref)
    first = pl.program_id(1) == 0
    ones_h = sel_ref[SEL_ONES_H]
    x = x_ref[0]
    qkv = _silu(_causal_conv(x[:, 0:3 * GROUP_W], halo_ref[0][:, 0:3 * GROUP_W], cw_ref, first))
    z = x[:, 3 * GROUP_W:4 * GROUP_W]
    ba = x[:, 4 * GROUP_W:]
    q = qkv[:, 0:GROUP_W]
    k = qkv[:, GROUP_W:2 * GROUP_W]
    v = qkv[:, 2 * GROUP_W:3 * GROUP_W]
    q = q * lax.rsqrt(_mm(q * q, ones_h) + L2_EPS) * (HEAD_DIM ** -0.5)
    k = k * lax.rsqrt(_mm(k * k, ones_h) + L2_EPS)
    beta_pad = _sigmoid(ba)
    g_pad = -jnp.exp(alog_ref[...]) * _softplus(ba + dtb_ref[...])
    beta_b = _mm_sel_r2(beta_pad, exp_ref[0])
    g_b = _mm_sel_r(g_pad, exp_ref[1])
    gc = _mm_sel_l(sel_ref[SEL_LTRI], g_b)
    eg = jnp.exp(gc)
    kb = k * beta_b
    q_bf = q.astype(BF16)
    k_bf = k.astype(BF16)
    kb_bf = kb.astype(BF16)
    vb_bf = (v * beta_b).astype(BF16)
    kbg_bf = (kb * eg).astype(BF16)
    qg_bf = (q * eg).astype(BF16)
    same = bdb_ref[BDB_SAME]
    chunks = [_chunk_rows(c) for c in range(N_CHUNKS)]
    neg = cat_ref[CAT_NEG]
    strict = cat_ref[CAT_STRICT]
    rforms = _row_form([gc[rs] for rs in chunks], bdb_ref, pick_ref)
    decays = [jnp.exp(gc[rs] - rf + neg) for rs, rf in zip(chunks, rforms)]
    k_rows = [_rows(k_bf[rs], same) for rs in chunks]
    a_cats = [(_dg(kb_bf[rs], kr, 1, 1) * dc * strict).astype(BF16)
              for rs, kr, dc in zip(chunks, k_rows, decays)]
    qks = [(_dg(q_bf[rs], kr, 1, 1) * dc).astype(BF16) for rs, kr, dc in zip(chunks, k_rows, decays)]
    t_cats = _inv_unit_lower(a_cats, catb_ref, same)
    uws = [_dg(t_cat, jnp.concatenate([_rows(vb_bf[rs], same), _rows(kbg_bf[rs], same)], axis=1), 1, 0)
           for rs, t_cat in zip(chunks, t_cats)]
    lasts = [gc[rs][CHUNK - 1:CHUNK, :] for rs in chunks]
    k_decs = [(k[rs] * jnp.exp(last - gc[rs])).astype(BF16) for rs, last in zip(chunks, lasts)]
    same_f = bdf_ref[BDF_SAME]
    us = [uw[:, 0:GROUP_W] for uw in uws]
    ws_bf = [uw[:, GROUP_W:].astype(BF16) for uw in uws]
    m_st = [(-same_f * _dg(kd, w, 0, 0)).astype(BF16) for kd, w in zip(k_decs, ws_bf)]
    c_st = [same_f * _dg(kd, u.astype(BF16), 0, 0) for kd, u in zip(k_decs, us)]
    st = st_ref[...]
    sts = []
    for c in range(N_CHUNKS):
        st_bf = st.astype(BF16)
        sts.append(st_bf)
        st = st * jnp.exp(lasts[c]) + _dg(m_st[c], st_bf, 1, 0) + c_st[c]
    st_ref[...] = st
    v_news = [(u - _dg(w, s_bf, 1, 0)).astype(BF16) for u, w, s_bf in zip(us, ws_bf, sts)]
    os_ = [_dg(qg_bf[rs], s_bf, 1, 0) + _dg(qk, _rows(vn, same), 1, 0)
           for rs, s_bf, qk, vn in zip(chunks, sts, qks, v_news)]
    o = jnp.concatenate(os_, axis=0)
    ms = _mm(o * o, ones_h) * (1.0 / HEAD_DIM)
    o_ref[0] = (o * lax.rsqrt(ms + NORM_EPS) * nw_ref[...] * _silu(z)).astype(o_ref.dtype)


RWKV_W = 896


def _rwkv_kernel(x_ref, halo_ref, bdf_ref, bdb_ref, cat_ref, sel_ref, exp_ref, pick_ref, catb_ref,
                 mu_ref, w0_ref, wup_ref, a0_ref, aup_ref, gup_ref, kk_ref, ka_ref,
                 rk_ref, lnw_ref, lnb_ref, o_ref, st_ref):
    _reset_state(st_ref)
    first = pl.program_id(1) == 0
    ones_h = sel_ref[SEL_ONES_H]
    x = x_ref[0]
    halo = jnp.where(first, 0.0, halo_ref[0])
    shifted = pltpu.roll(jnp.concatenate([halo, x], axis=0), 1, axis=0)[HALO:]
    x = x + mu_ref[...] * (shifted - x)
    r = x[:, 0:GROUP_W]
    k = x[:, GROUP_W:2 * GROUP_W]
    v = x[:, 2 * GROUP_W:3 * GROUP_W]
    lora = x[:, 3 * GROUP_W:]
    w_log = -_softplus(-(w0_ref[...] + _mm(jnp.tanh(lora), wup_ref[...]))) - RWKV_DECAY_OFFSET
    lw = -jnp.exp(w_log)
    a_gate = _sigmoid(a0_ref[...] + _mm(lora, aup_ref[...]))
    g = _mm(_sigmoid(lora), gup_ref[...])
    kk = k * kk_ref[...]
    kk = kk * lax.rsqrt(_mm(kk * kk, ones_h) + L2_EPS)
    k = k * (1.0 + (a_gate - 1.0) * ka_ref[...])
    b_v = kk * a_gate
    cl = _mm_sel_l(sel_ref[SEL_LTRI], lw)
    e_in = jnp.exp(cl)
    e_inv = jnp.exp(-cl)
    at_bf = (-kk * jnp.exp(cl - lw)).astype(BF16)
    rt_bf = (r * e_in).astype(BF16)
    bt_bf = (b_v * e_inv).astype(BF16)
    kt_bf = (k * e_inv).astype(BF16)
    v_bf = v.astype(BF16)
    same = bdb_ref[BDB_SAME]
    strict = cat_ref[CAT_STRICT]
    incl = cat_ref[CAT_INCL]
    chunks = [_chunk_rows(c) for c in range(N_CHUNKS)]
    zero = jnp.zeros((), F32)
    b_rows = [_rows(bt_bf[rs], same) for rs in chunks]
    k_rows = [_rows(kt_bf[rs], same) for rs in chunks]
    v_rows = [_rows(v_bf[rs], same) for rs in chunks]
    n_cats = [jnp.where(strict > 0, -_dg(at_bf[rs], br, 1, 1), zero).astype(BF16)
              for rs, br in zip(chunks, b_rows)]
    aks = [jnp.where(strict > 0, _dg(at_bf[rs], kr, 1, 1), zero).astype(BF16)
           for rs, kr in zip(chunks, k_rows)]
    rbs = [jnp.where(incl > 0, _dg(rt_bf[rs], br, 1, 1), zero).astype(BF16)
           for rs, br in zip(chunks, b_rows)]
    rks = [jnp.where(incl > 0, _dg(rt_bf[rs], kr, 1, 1), zero).astype(BF16)
           for rs, kr in zip(chunks, k_rows)]
    t_cats = _inv_unit_lower(n_cats, catb_ref, same)
    tas = [_dg(t_cat, _rows(at_bf[rs], same), 1, 0).astype(BF16) for rs, t_cat in zip(chunks, t_cats)]
    tvs = [_dg(t_cat, _rows(_dg(ak, vr, 1, 0).astype(BF16), same), 1, 0)
           for t_cat, ak, vr in zip(t_cats, aks, v_rows)]
    y_vs = [_dg(rk, vr, 1, 0) for rk, vr in zip(rks, v_rows)]
    lasts = [cl[rs][CHUNK - 1:CHUNK, :] for rs in chunks]
    to_ends = [jnp.exp(last - cl[rs]) for rs, last in zip(chunks, lasts)]
    same_f = bdf_ref[BDF_SAME]
    b_ends = [(b_v[rs] * te).astype(BF16) for rs, te in zip(chunks, to_ends)]
    k_ends = [(k[rs] * te).astype(BF16) for rs, te in zip(chunks, to_ends)]
    m_st = [(same_f * _dg(ta, be, 0, 0)).astype(BF16) for ta, be in zip(tas, b_ends)]
    c_st = [same_f * _dg(jnp.concatenate([tv.astype(BF16), v_bf[rs]], axis=0),
                         jnp.concatenate([be, ke], axis=0), 0, 0)
            for tv, rs, be, ke in zip(tvs, chunks, b_ends, k_ends)]
    st = st_ref[...]
    sts = []
    for c in range(N_CHUNKS):
        st_bf = st.astype(BF16)
        sts.append(st_bf)
        st = st * jnp.exp(lasts[c]) + _dg(st_bf, m_st[c], 1, 0) + c_st[c]
    st_ref[...] = st
    us = [(_dg(ta, s_bf, 1, 1) + tv).astype(BF16) for ta, tv, s_bf in zip(tas, tvs, sts)]
    ys = [_dg(rt_bf[rs], s_bf, 1, 1) + _dg(rb, _rows(u, same), 1, 0) + yv
          for rs, s_bf, rb, u, yv in zip(chunks, sts, rbs, us, y_vs)]
    y = jnp.concatenate(ys, axis=0)
    inv_d = 1.0 / HEAD_DIM
    mean = _mm(y, ones_h) * inv_d
    yc = y - mean
    var = _mm(yc * yc, ones_h) * inv_d
    yn = yc * lax.rsqrt(var + RWKV_GN_EPS) * lnw_ref[...] + lnb_ref[...]
    bonus = _mm(r * k * rk_ref[...], ones_h) * v
    o_ref[0] = ((yn + bonus) * g).astype(o_ref.dtype)


SB_BLOCK = 256
SB_NEGLIGIBLE = 128.0


def _neg_abs(x):
    sign = jnp.uint32(0x80000000)
    return lax.bitcast_convert_type(lax.bitcast_convert_type(x, jnp.uint32) | sign, F32)


def _sb_kernel(q_ref, k_ref, v_ref, o_ref, acc_ref, run_ref, zz_ref, w_ref):
    qi = pl.program_id(1)
    t = SB_BLOCK
    heads = range(N_HEADS)
    scale = HEAD_DIM ** -0.5
    assert scale == 0.125
    lane_head = _iota((t, GROUP_W), 1) // HEAD_DIM
    q = q_ref[0] * jnp.asarray(scale, q_ref.dtype)
    zero = jnp.zeros((), q.dtype)
    q_h = [jnp.where(lane_head == h, q, zero) for h in heads]
    tri = (_iota((t, t), 0) >= _iota((t, t), 1)).astype(BF16)
    upper2 = jnp.concatenate([tri, tri], axis=0)
    causal = _iota((t, t), 1) < _iota((t, t), 0)
    acc_ref[...] = jnp.zeros_like(acc_ref)
    run_ref[...] = jnp.zeros_like(run_ref)

    def scores(j):
        kj = k_ref[0, pl.ds(pl.multiple_of(j * t, t), t), :]
        return [_dg(q_h[h], kj, 1, 1) for h in heads]

    def weights(zz, masked):
        parts, sums = [], []
        for h in heads:
            sp = jnp.maximum(zz[h], 0.0) + jnp.log(1.0 + jnp.exp(_neg_abs(zz[h])))
            if masked:
                sp = jnp.where(causal, sp, 0.0)
            hi, lo = _split2(sp)
            parts.append(jnp.concatenate([hi, lo], axis=1))
            sums.append(jnp.sum(sp, axis=1, keepdims=True))
        later = [_dg(p, upper2, 1, 0) for p in parts]
        wgts = []
        for h in heads:
            run = run_ref[h]
            wgt = jnp.exp(zz[h] - later[h] - jnp.concatenate([run] * (t // LANE), axis=1))
            if masked:
                wgt = jnp.where(causal, wgt, 0.0)
            wgts.append(wgt.astype(BF16))
            run_ref[h] = run + sums[h]
        return jnp.concatenate(wgts, axis=1)

    def accumulate(w_cat, j):
        vj = v_ref[0, pl.ds(pl.multiple_of(j * t, t), t), :]
        v_cat = jnp.concatenate([jnp.where(lane_head == h, vj, zero) for h in heads], axis=0)
        acc_ref[...] += _dg(w_cat, v_cat, 1, 0)

    def stash(zz, w_cat):
        for h in heads:
            zz_ref[h] = zz[h]
        w_ref[...] = w_cat

    zz_next = scores(jnp.maximum(qi - 1, 0))
    stash(zz_next, weights(scores(qi), True))

    def run_min():
        r = run_ref[0]
        for h in range(1, N_HEADS):
            r = jnp.minimum(r, run_ref[h])
        return jnp.min(r)

    def cond(carry):
        i, rmin = carry
        return jnp.logical_and(i <= qi, rmin < SB_NEGLIGIBLE)

    def body(carry):
        i, _ = carry
        j = qi - i
        accumulate(w_ref[...], j + 1)
        zz_next = scores(jnp.maximum(j - 1, 0))
        stash(zz_next, weights([zz_ref[h] for h in heads], False))
        return i + 1, run_min()

    i_end, _ = lax.while_loop(cond, body, (jnp.int32(1), run_min()))
    accumulate(w_ref[...], qi - (i_end - 1))
    o_ref[0] = acc_ref[...].astype(o_ref.dtype)


def _sb(pd):
    bsz, s, _ = pd.shape
    t = SB_BLOCK
    return pl.pallas_call(
        _sb_kernel,
        grid=(bsz, s // t),
        in_specs=[pl.BlockSpec((1, t, GROUP_W), lambda b, i: (b, i, 0)),
                  pl.BlockSpec((1, s, GROUP_W), lambda b, i: (b, 0, 1)),
                  pl.BlockSpec((1, s, GROUP_W), lambda b, i: (b, 0, 2))],
        out_specs=pl.BlockSpec((1, t, GROUP_W), lambda b, i: (b, i, 0)),
        out_shape=jax.ShapeDtypeStruct((bsz, s, GROUP_W), BF16),
        scratch_shapes=[pltpu.VMEM((t, GROUP_W), F32), pltpu.VMEM((N_HEADS, t, LANE), F32),
                        pltpu.VMEM((N_HEADS, t, t), F32), pltpu.VMEM((t, N_HEADS * t), BF16)],
        compiler_params=_params(("parallel", "arbitrary")),
        name="stickbreak",
    )(pd, pd, pd)


def _row(v, width=None):
    v = v.reshape(1, -1).astype(F32)
    if width is not None and v.shape[1] < width:
        v = jnp.pad(v, ((0, 0), (0, width - v.shape[1])))
    return v


def _pad_cols(w, width):
    return jnp.pad(w, ((0, 0), (0, width - w.shape[1])))


def _pad_rows_at(w, offset, total):
    return jnp.pad(w, ((offset, total - offset - w.shape[0]), (0, 0)))


def _mixer(h2, bsz, s, gain, w_in, ssm_conv_w, ssm_conv_b, ssm_dt_bias, ssm_a_log, ssm_d, ssm_norm,
           rwkv_mu, rwkv_w0, rwkv_w_up, rwkv_a0, rwkv_a_up, rwkv_g_up, rwkv_k_k, rwkv_k_a, rwkv_r_k,
           rwkv_ln_w, rwkv_ln_b, gdn_conv_w, gdn_a_log, gdn_dt_bias, gdn_norm, tm):
    gw = GROUP_W
    ssm_in = 4 * gw + N_HEADS
    c0 = ssm_in
    c1 = c0 + RWKV_W
    c2 = c1 + 4 * gw + 2 * N_HEADS
    wa = _pad_cols(w_in[:, :c0], SSD_W).astype(BF16)
    wb = w_in[:, c0:c1].astype(BF16)
    wc = _pad_cols(w_in[:, c1:c2], GDN_W).astype(BF16)
    wd = w_in[:, c2:].astype(BF16)
    pa, pb, pc, pd = _inproj(h2, _row(gain), [wa, wb, wc, wd], tm)

    ya = _mixer_call(_ssd_kernel, "ssd", pa.reshape(bsz, s, SSD_W),
                     [ssm_conv_w.astype(F32), _row(ssm_conv_b), _row(ssm_dt_bias, LANE),
                      _row(ssm_a_log, LANE), _row(jnp.repeat(ssm_d, HEAD_DIM)), _row(ssm_norm)])

    lora_w = RWKV_W - 3 * gw
    yb = _mixer_call(_rwkv_kernel, "rwkv", pb.reshape(bsz, s, RWKV_W),
                     [_row(rwkv_mu), _row(rwkv_w0),
                      _pad_rows_at(rwkv_w_up, 0, lora_w).astype(BF16), _row(rwkv_a0),
                      _pad_rows_at(rwkv_a_up, rwkv_w_up.shape[0], lora_w).astype(BF16),
                      _pad_rows_at(rwkv_g_up, rwkv_w_up.shape[0] + rwkv_a_up.shape[0], lora_w).astype(BF16),
                      _row(rwkv_k_k), _row(rwkv_k_a), _row(rwkv_r_k), _row(rwkv_ln_w), _row(rwkv_ln_b)])

    pad4 = lambda t: jnp.pad(t.reshape(1, -1).astype(F32), ((0, 0), (N_HEADS, LANE - 2 * N_HEADS)))
    yc = _mixer_call(_gdn_kernel, "gdn", pc.reshape(bsz, s, GDN_W),
                     [gdn_conv_w.astype(F32), pad4(gdn_a_log), pad4(gdn_dt_bias),
                      _row(jnp.tile(gdn_norm, N_HEADS))])

    yd = _sb(pd.reshape(bsz, s, 3 * gw))

    n = bsz * s
    return [ya.reshape(n, gw), yb.reshape(n, gw), yc.reshape(n, gw), yd.reshape(n, gw)]


def kernel(x, p, ffn1_norm, ffn1_w_gate, ffn1_w_up, ffn1_w_down, mix_norm, w_in, ssm_conv_w, ssm_conv_b, ssm_dt_bias, ssm_a_log, ssm_d, ssm_norm, rwkv_mu, rwkv_w0, rwkv_w_up, rwkv_a0, rwkv_a_up, rwkv_g_up, rwkv_k_k, rwkv_k_a, rwkv_r_k, rwkv_ln_w, rwkv_ln_b, gdn_conv_w, gdn_a_log, gdn_dt_bias, gdn_norm, w_out, ffn2_norm, ffn2_w_gate, ffn2_w_up, ffn2_w_down, ple_norm, ple_w_gate, ple_w_proj, final_norm):
    bsz, s, d = x.shape
    depth = p.shape[0]
    n = bsz * s
    tm = min(512, n)
    h = x.reshape(n, d)
    p3 = p.reshape(depth, n, p.shape[-1])
    for i in range(depth):
        h = _ffn(h, _row(ffn1_norm[i]), ffn1_w_gate[i].astype(BF16), ffn1_w_up[i].astype(BF16),
                 ffn1_w_down[i].astype(BF16), tm)
        ys = _mixer(h, bsz, s, mix_norm[i], w_in[i], ssm_conv_w[i], ssm_conv_b[i], ssm_dt_bias[i],
                    ssm_a_log[i], ssm_d[i], ssm_norm[i], rwkv_mu[i], rwkv_w0[i], rwkv_w_up[i],
                    rwkv_a0[i], rwkv_a_up[i], rwkv_g_up[i], rwkv_k_k[i], rwkv_k_a[i], rwkv_r_k[i],
                    rwkv_ln_w[i], rwkv_ln_b[i], gdn_conv_w[i], gdn_a_log[i], gdn_dt_bias[i],
                    gdn_norm[i], tm)
        h = _post(h, ys, p3, i,
                  [w_out[i].astype(BF16), _row(ffn2_norm[i]), ffn2_w_gate[i].astype(BF16),
                   ffn2_w_up[i].astype(BF16), ffn2_w_down[i].astype(BF16), _row(ple_norm[i]),
                   ple_w_gate[i].astype(BF16), ple_w_proj[i].astype(BF16), _row(final_norm)],
                  i == depth - 1, tm)
    return h.reshape(bsz, s, d)
```

```python
import functools

import numpy as np
import jax
import jax.numpy as jnp
from jax import lax
from jax.experimental import pallas as pl
from jax.experimental.pallas import tpu as pltpu

F32 = jnp.float32
BF16 = jnp.bfloat16

HEAD_DIM = 64
N_HEADS = 4
GROUP_W = HEAD_DIM * N_HEADS
CHUNK = 64
N_CHUNKS = 4
CONV_K = 4
SSM_STATE = 128
NORM_EPS = 1e-6
L2_EPS = 1e-6
RWKV_GN_EPS = 64e-5
RWKV_DECAY_OFFSET = 0.5
HALO = 8
LANE = 128
VMEM_LIMIT = 56 * 1024 * 1024


def _dg(a, b, ca, cb):
    return lax.dot_general(a, b, (((ca,), (cb,)), ((), ())), preferred_element_type=F32)


_DIMS = {"nn": (1, 0), "nt": (1, 1), "tn": (0, 0)}


def _mm(a, b, kind="nn"):
    ca, cb = _DIMS[kind]
    return _dg(a.astype(BF16), b.astype(BF16), ca, cb)


def _split2(x):
    hi = x.astype(BF16)
    lo = (x - hi.astype(F32)).astype(BF16)
    return hi, lo


def _split3(x):
    hi = x.astype(BF16)
    r = x - hi.astype(F32)
    mid = r.astype(BF16)
    lo = (r - mid.astype(F32)).astype(BF16)
    return hi, mid, lo


def _mm_sel_l(sel, b, kind="nn"):
    ca, cb = _DIMS[kind]
    h, m, l = _split3(b)
    return _dg(sel, h, ca, cb) + (_dg(sel, m, ca, cb) + _dg(sel, l, ca, cb))


def _mm_sel_r(a, sel, kind="nn"):
    ca, cb = _DIMS[kind]
    h, m, l = _split3(a)
    return _dg(h, sel, ca, cb) + (_dg(m, sel, ca, cb) + _dg(l, sel, ca, cb))


def _mm_sel_r2(a, sel):
    h, l = _split2(a)
    return _dg(h, sel, 1, 0) + _dg(l, sel, 1, 0)


def _sigmoid(x):
    return 1.0 / (1.0 + jnp.exp(-x))


def _silu(x):
    return x * _sigmoid(x)


def _softplus(x):
    return jnp.maximum(x, 0.0) + jnp.log1p(jnp.exp(-jnp.abs(x)))


def _iota(shape, dim):
    return lax.broadcasted_iota(jnp.int32, shape, dim)


def _tile_rows(x, n):
    return jnp.concatenate([x] * n, axis=0)


def _rms(x, gain_row):
    ms = jnp.mean(x * x, axis=-1, keepdims=True)
    return x * lax.rsqrt(ms + NORM_EPS) * gain_row


BDF_SAME, BDF_GRP_STATE = range(2)
BDB_SAME, BDB_GRP_ROWS = range(2)
CAT_NEG, CAT_STRICT, CAT_INCL = range(3)
CATB_EYE, CATB_M8, CATB_OFF8, CATB_OFF16, CATB_OFF32 = range(5)
SEL_ONES_H, SEL_ONES_G, SEL_LTRI = range(3)


def _np_consts():
    r = np.arange(GROUP_W)[:, None]
    c = np.arange(GROUP_W)[None, :]
    same = (r // HEAD_DIM) == (c // HEAD_DIM)

    bdf = np.stack([same, (r // SSM_STATE) == ((c // HEAD_DIM) // 2)]).astype(np.float32)
    bdb = np.stack([same, ((r // HEAD_DIM) // 2) == (c // SSM_STATE)]).astype(np.float32)
    l = np.arange(CHUNK)[:, None]
    s = np.arange(GROUP_W)[None, :] % CHUNK
    cat = np.stack([np.where(l >= s, 0.0, -np.inf), l > s, l >= s]).astype(np.float32)

    def off(b):
        return ((l // (2 * b)) == (s // (2 * b))) & (((l // b) % 2) == 1) & (((s // b) % 2) == 0)

    catb = np.stack([l == s, (l // 8) == (s // 8), off(8), off(16), off(32)]).astype(np.float32)
    t = np.arange(N_CHUNKS * CHUNK)
    ltri = (t[:, None] >= t[None, :]) & ((t[:, None] // CHUNK) == (t[None, :] // CHUNK))
    sel = np.stack([same, (r // SSM_STATE) == (c // SSM_STATE), ltri]).astype(np.float32)
    er = np.arange(LANE)[:, None]
    ec = np.arange(GROUP_W)[None, :] // HEAD_DIM
    expand = np.stack([er == ec, er == ec + N_HEADS]).astype(np.float32)
    pick = np.broadcast_to(np.arange(GROUP_W)[None, :] % HEAD_DIM == 0, (CHUNK, GROUP_W)).astype(np.float32)
    return (jnp.asarray(bdf), jnp.asarray(bdb, BF16), jnp.asarray(cat), jnp.asarray(sel, BF16),
            jnp.asarray(expand, BF16), jnp.asarray(pick, BF16), jnp.asarray(catb, BF16))


def _rows(x_bf, mask_bf):
    return _tile_rows(x_bf, N_HEADS) * mask_bf


def _row_form(colvals, bdb_ref, pick_ref):
    same = bdb_ref[BDB_SAME]
    pick = pick_ref[...]
    parts = [_split3(x) for x in colvals]
    prods = [[_dg(pick, _rows(p, same), 1, 1) for p in ps] for ps in parts]
    return [h + (m + l) for h, m, l in prods]


def _inv_unit_lower(n_cats_bf, catb_ref, same):
    def mul(a_cat_bf, b_cat_bf):
        return _dg(a_cat_bf, _rows(b_cat_bf, same), 1, 0)

    eye = catb_ref[CATB_EYE]
    m8 = catb_ref[CATB_M8]
    d = [n * m8 for n in n_cats_bf]
    d2 = [mul(x, x).astype(BF16) for x in d]
    d4 = [mul(x, x).astype(BF16) for x in d2]
    t = [mul(eye - x, eye + y) for x, y in zip(d, d2)]
    t = [mul(x.astype(BF16), eye + y) for x, y in zip(t, d4)]
    for idx in (CATB_OFF8, CATB_OFF16, CATB_OFF32):
        off = catb_ref[idx]
        tb = [x.astype(BF16) for x in t]
        to = [mul(x, n * off).astype(BF16) for x, n in zip(tb, n_cats_bf)]
        t = [x - mul(y, z) for x, y, z in zip(t, to, tb)]
    return [x.astype(BF16) for x in t]


def _causal_conv(cur, halo, w_ref, first):
    halo = jnp.where(first, 0.0, halo)
    ext = jnp.concatenate([halo, cur], axis=0)
    acc = cur * w_ref[CONV_K - 1:CONV_K, :]
    for j in range(CONV_K - 1):
        sh = pltpu.roll(ext, CONV_K - 1 - j, axis=0)[HALO:]
        acc = acc + sh * w_ref[j:j + 1, :]
    return acc


def _chunk_rows(c):
    return slice(c * CHUNK, (c + 1) * CHUNK)


def _ffn_kernel(h_ref, g_ref, wg_ref, wu_ref, wd_ref, o_ref):
    x = h_ref[...]
    u = _rms(x, g_ref[...]).astype(BF16)
    a = jnp.dot(u, wg_ref[...], preferred_element_type=F32)
    b = jnp.dot(u, wu_ref[...], preferred_element_type=F32)
    act = (_silu(a) * b).astype(BF16)
    y = jnp.dot(act, wd_ref[...], preferred_element_type=F32)
    o_ref[...] = x + 0.5 * y


def _const_spec(shape):
    nd = len(shape)
    return pl.BlockSpec(shape, lambda *_: (0,) * nd, pipeline_mode=pl.Buffered(1))


def _params(sem):
    return pltpu.CompilerParams(dimension_semantics=sem, vmem_limit_bytes=VMEM_LIMIT)


def _ffn(h, gain, wg, wu, wd, tm):
    n, d = h.shape
    f = wg.shape[1]
    return pl.pallas_call(
        _ffn_kernel,
        grid=(n // tm,),
        in_specs=[pl.BlockSpec((tm, d), lambda i: (i, 0)), _const_spec((1, d)),
                  _const_spec((d, f)), _const_spec((d, f)), _const_spec((f, d))],
        out_specs=pl.BlockSpec((tm, d), lambda i: (i, 0)),
        out_shape=jax.ShapeDtypeStruct((n, d), F32),
        compiler_params=_params(("parallel",)),
        name="ffn",
    )(h, gain, wg, wu, wd)


def _inproj_kernel(h_ref, g_ref, wa_ref, wb_ref, wc_ref, wd_ref, oa_ref, ob_ref, oc_ref, od_ref):
    u = _rms(h_ref[...], g_ref[...]).astype(BF16)
    for w_ref, o_ref in ((wa_ref, oa_ref), (wb_ref, ob_ref), (wc_ref, oc_ref), (wd_ref, od_ref)):
        o_ref[...] = jnp.dot(u, w_ref[...], preferred_element_type=F32).astype(o_ref.dtype)


def _inproj(h, gain, ws, tm):
    n, d = h.shape
    widths = [w.shape[1] for w in ws]
    dtypes = [F32, F32, F32, BF16]
    return pl.pallas_call(
        _inproj_kernel,
        grid=(n // tm,),
        in_specs=[pl.BlockSpec((tm, d), lambda i: (i, 0)), _const_spec((1, d))]
                 + [_const_spec((d, wd)) for wd in widths],
        out_specs=[pl.BlockSpec((tm, wd), lambda i: (i, 0)) for wd in widths],
        out_shape=[jax.ShapeDtypeStruct((n, wd), dt) for wd, dt in zip(widths, dtypes)],
        compiler_params=_params(("parallel",)),
        name="inproj",
    )(h, gain, *ws)


def _post_kernel(h_ref, ya_ref, yb_ref, yc_ref, yd_ref, p_ref, wo_ref, g2_ref, wg_ref, wu_ref, wd_ref,
                 gp_ref, wpg_ref, wpp_ref, fg_ref, o_ref, *, final):
    x = h_ref[...]
    for i, y_ref in enumerate((ya_ref, yb_ref, yc_ref, yd_ref)):
        x = x + jnp.dot(y_ref[...], wo_ref[i * GROUP_W:(i + 1) * GROUP_W, :], preferred_element_type=F32)
    u = _rms(x, g2_ref[...]).astype(BF16)
    a = jnp.dot(u, wg_ref[...], preferred_element_type=F32)
    b = jnp.dot(u, wu_ref[...], preferred_element_type=F32)
    act = (_silu(a) * b).astype(BF16)
    x = x + 0.5 * jnp.dot(act, wd_ref[...], preferred_element_type=F32)
    u = _rms(x, gp_ref[...]).astype(BF16)
    gate = _sigmoid(jnp.dot(u, wpg_ref[...], preferred_element_type=F32))
    e = jnp.dot(p_ref[...].astype(BF16), wpp_ref[...], preferred_element_type=F32)
    y = x + e * gate
    if final:
        y = _rms(y, fg_ref[...])
    o_ref[...] = y


def _post(h, ys, p3, layer, consts, final, tm):
    n, d = h.shape
    pd = p3.shape[2]
    return pl.pallas_call(
        functools.partial(_post_kernel, final=final),
        grid=(n // tm,),
        in_specs=[pl.BlockSpec((tm, d), lambda i: (i, 0))]
                 + [pl.BlockSpec((tm, GROUP_W), lambda i: (i, 0)) for _ in ys]
                 + [pl.BlockSpec((None, tm, pd), lambda i: (layer, i, 0))]
                 + [_const_spec(c.shape) for c in consts],
        out_specs=pl.BlockSpec((tm, d), lambda i: (i, 0)),
        out_shape=jax.ShapeDtypeStruct((n, d), F32),
        compiler_params=_params(("parallel",)),
        name="post",
    )(h, *ys, p3, *consts)


def _halo_map(rows_per_block):
    step = rows_per_block // HALO
    return lambda b, i: (b, jnp.maximum(i * step - 1, 0), 0)


N_CONSTS = 7


def _mixers_kernel(*refs, bodies, n_params):
    k = len(bodies)
    consts = refs[2 * k:2 * k + N_CONSTS]
    pos = 2 * k + N_CONSTS
    params = []
    for n in n_params:
        params.append(refs[pos:pos + n])
        pos += n
    outs = refs[pos:pos + k]
    states = refs[pos + k:pos + 2 * k]

    @pl.when(pl.program_id(1) == 0)
    def _():
        for st_ref in states:
            st_ref[...] = jnp.zeros_like(st_ref)

    gens = [body(refs[2 * i], refs[2 * i + 1], *consts, *params[i], outs[i], states[i])
            for i, body in enumerate(bodies)]
    wanted = [next(g) for g in gens]
    flat = [n for w in wanted for n in w]
    bdb_ref, catb_ref = consts[1], consts[6]
    t_cats = _inv_unit_lower(flat, catb_ref, bdb_ref[BDB_SAME]) if flat else []
    pos = 0
    for g, w in zip(gens, wanted):
        try:
            g.send(t_cats[pos:pos + len(w)])
        except StopIteration:
            pos += len(w)
        else:
            raise AssertionError("a mixer body yields exactly once")


def _mixers_call(name, mixers):
    bsz, s, _ = mixers[0][1].shape
    t = N_CHUNKS * CHUNK
    consts = _np_consts()
    assert len(consts) == N_CONSTS
    in_specs, args = [], []
    for _, x, _ in mixers:
        w = x.shape[2]
        in_specs += [pl.BlockSpec((1, t, w), lambda b, i: (b, i, 0)), pl.BlockSpec((1, HALO, w), _halo_map(t))]
        args += [x, x]
    in_specs += [_const_spec(c.shape) for c in consts]
    args += list(consts)
    for _, _, params in mixers:
        in_specs += [_const_spec(p.shape) for p in params]
        args += list(params)
    k = len(mixers)
    return pl.pallas_call(
        functools.partial(_mixers_kernel, bodies=[m[0] for m in mixers],
                          n_params=[len(m[2]) for m in mixers]),
        grid=(bsz, s // t),
        in_specs=in_specs,
        out_specs=[pl.BlockSpec((1, t, GROUP_W), lambda b, i: (b, i, 0))] * k,
        out_shape=[jax.ShapeDtypeStruct((bsz, s, GROUP_W), BF16)] * k,
        scratch_shapes=[pltpu.VMEM((GROUP_W, GROUP_W), F32)] * k,
        compiler_params=_params(("parallel", "arbitrary")),
        name=name,
    )(*args)


SSD_W = 1152


def _ssd_body(x_ref, halo_ref, bdf_ref, bdb_ref, cat_ref, sel_ref, exp_ref, pick_ref, catb_ref,
              cw_ref, cb_ref, dtb_ref, alog_ref, dsk_ref, nw_ref, o_ref, st_ref):
    yield []
    first = pl.program_id(1) == 0
    x = x_ref[0]
    z = x[:, 0:GROUP_W]
    xbc = _causal_conv(x[:, GROUP_W:4 * GROUP_W], halo_ref[0][:, GROUP_W:4 * GROUP_W], cw_ref, first)
    xbc = _silu(xbc + cb_ref[...])
    xs = xbc[:, 0:GROUP_W]
    bm = xbc[:, GROUP_W:2 * GROUP_W].astype(BF16)
    cm = xbc[:, 2 * GROUP_W:3 * GROUP_W].astype(BF16)
    dt_pad = _softplus(x[:, 4 * GROUP_W:] + dtb_ref[...])
    la_pad = dt_pad * (-jnp.exp(alog_ref[...]))
    dt_b = _mm_sel_r(dt_pad, exp_ref[0])
    la_b = _mm_sel_r(la_pad, exp_ref[0])
    cs = _mm_sel_l(sel_ref[SEL_LTRI], la_b)
    xc = xs * dt_b
    ecs = jnp.exp(cs)
    same = bdb_ref[BDB_SAME]
    grp_rows = bdb_ref[BDB_GRP_ROWS]
    grp_state = bdf_ref[BDF_GRP_STATE]
    chunks = [_chunk_rows(c) for c in range(N_CHUNKS)]
    neg = cat_ref[CAT_NEG]
    rforms = _row_form([cs[rs] for rs in chunks], bdb_ref, pick_ref)
    segs = [jnp.exp(cs[rs] - rf + neg) for rs, rf in zip(chunks, rforms)]
    scores = [(_dg(cm[rs], _rows(bm[rs], grp_rows), 1, 1) * sg).astype(BF16)
              for rs, sg in zip(chunks, segs)]
    xc_bf = xc.astype(BF16)
    ys = [_dg(sc, _rows(xc_bf[rs], same), 1, 0) for rs, sc in zip(chunks, scores)]
    lasts = [cs[rs][CHUNK - 1:CHUNK, :] for rs in chunks]
    upds = [_dg(bm[rs], (xc[rs] * jnp.exp(last - cs[rs])).astype(BF16), 0, 0) * grp_state
            for rs, last in zip(chunks, lasts)]
    st = st_ref[...]
    for c, rs in enumerate(chunks):
        ys[c] = ys[c] + _dg(cm[rs], st.astype(BF16), 1, 0) * ecs[rs]
        st = st * jnp.exp(lasts[c]) + upds[c]
    st_ref[...] = st
    y = (jnp.concatenate(ys, axis=0) + dsk_ref[...] * xs) * _silu(z)
    ms = _mm(y * y, sel_ref[SEL_ONES_G]) * (1.0 / SSM_STATE)
    o_ref[0] = (y * lax.rsqrt(ms + NORM_EPS) * nw_ref[...]).astype(o_ref.dtype)


GDN_W = 1152


def _gdn_body(x_ref, halo_ref, bdf_ref, bdb_ref, cat_ref, sel_ref, exp_ref, pick_ref, catb_ref,
              cw_ref, alog_ref, dtb_ref, nw_ref, o_ref, st_ref):
    first = pl.program_id(1) == 0
    ones_h = sel_ref[SEL_ONES_H]
    x = x_ref[0]
    qkv = _silu(_causal_conv(x[:, 0:3 * GROUP_W], halo_ref[0][:, 0:3 * GROUP_W], cw_ref, first))
    z = x[:, 3 * GROUP_W:4 * GROUP_W]
    ba = x[:, 4 * GROUP_W:]
    q = qkv[:, 0:GROUP_W]
    k = qkv[:, GROUP_W:2 * GROUP_W]
    v = qkv[:, 2 * GROUP_W:3 * GROUP_W]
    q = q * lax.rsqrt(_mm(q * q, ones_h) + L2_EPS) * (HEAD_DIM ** -0.5)
    k = k * lax.rsqrt(_mm(k * k, ones_h) + L2_EPS)
    beta_pad = _sigmoid(ba)
    g_pad = -jnp.exp(alog_ref[...]) * _softplus(ba + dtb_ref[...])
    beta_b = _mm_sel_r2(beta_pad, exp_ref[0])
    g_b = _mm_sel_r(g_pad, exp_ref[1])
    gc = _mm_sel_l(sel_ref[SEL_LTRI], g_b)
    eg = jnp.exp(gc)
    kb = k * beta_b
    q_bf = q.astype(BF16)
    k_bf = k.astype(BF16)
    kb_bf = kb.astype(BF16)
    vb_bf = (v * beta_b).astype(BF16)
    kbg_bf = (kb * eg).astype(BF16)
    qg_bf = (q * eg).astype(BF16)
    same = bdb_ref[BDB_SAME]
    chunks = [_chunk_rows(c) for c in range(N_CHUNKS)]
    neg = cat_ref[CAT_NEG]
    strict = cat_ref[CAT_STRICT]
    rforms = _row_form([gc[rs] for rs in chunks], bdb_ref, pick_ref)
    decays = [jnp.exp(gc[rs] - rf + neg) for rs, rf in zip(chunks, rforms)]
    k_rows = [_rows(k_bf[rs], same) for rs in chunks]
    a_cats = [(_dg(kb_bf[rs], kr, 1, 1) * dc * strict).astype(BF16)
              for rs, kr, dc in zip(chunks, k_rows, decays)]
    qks = [(_dg(q_bf[rs], kr, 1, 1) * dc).astype(BF16) for rs, kr, dc in zip(chunks, k_rows, decays)]
    t_cats = yield a_cats
    uws = [_dg(t_cat, jnp.concatenate([_rows(vb_bf[rs], same), _rows(kbg_bf[rs], same)], axis=1), 1, 0)
           for rs, t_cat in zip(chunks, t_cats)]
    lasts = [gc[rs][CHUNK - 1:CHUNK, :] for rs in chunks]
    k_decs = [(k[rs] * jnp.exp(last - gc[rs])).astype(BF16) for rs, last in zip(chunks, lasts)]
    same_f = bdf_ref[BDF_SAME]
    us = [uw[:, 0:GROUP_W] for uw in uws]
    ws_bf = [uw[:, GROUP_W:].astype(BF16) for uw in uws]
    m_st = [(-same_f * _dg(kd, w, 0, 0)).astype(BF16) for kd, w in zip(k_decs, ws_bf)]
    c_st = [same_f * _dg(kd, u.astype(BF16), 0, 0) for kd, u in zip(k_decs, us)]
    st = st_ref[...]
    sts = []
    for c in range(N_CHUNKS):
        st_bf = st.astype(BF16)
        sts.append(st_bf)
        st = st * jnp.exp(lasts[c]) + _dg(m_st[c], st_bf, 1, 0) + c_st[c]
    st_ref[...] = st
    v_news = [(u - _dg(w, s_bf, 1, 0)).astype(BF16) for u, w, s_bf in zip(us, ws_bf, sts)]
    os_ = [_dg(qg_bf[rs], s_bf, 1, 0) + _dg(qk, _rows(vn, same), 1, 0)
           for rs, s_bf, qk, vn in zip(chunks, sts, qks, v_news)]
    o = jnp.concatenate(os_, axis=0)
    ms = _mm(o * o, ones_h) * (1.0 / HEAD_DIM)
    o_ref[0] = (o * lax.rsqrt(ms + NORM_EPS) * nw_ref[...] * _silu(z)).astype(o_ref.dtype)


RWKV_W = 896


def _rwkv_body(x_ref, halo_ref, bdf_ref, bdb_ref, cat_ref, sel_ref, exp_ref, pick_ref, catb_ref,
               mu_ref, w0_ref, wup_ref, a0_ref, aup_ref, gup_ref, kk_ref, ka_ref,
               rk_ref, lnw_ref, lnb_ref, o_ref, st_ref):
    first = pl.program_id(1) == 0
    ones_h = sel_ref[SEL_ONES_H]
    x = x_ref[0]
    halo = jnp.where(first, 0.0, halo_ref[0])
    shifted = pltpu.roll(jnp.concatenate([halo, x], axis=0), 1, axis=0)[HALO:]
    x = x + mu_ref[...] * (shifted - x)
    r = x[:, 0:GROUP_W]
    k = x[:, GROUP_W:2 * GROUP_W]
    v = x[:, 2 * GROUP_W:3 * GROUP_W]
    lora = x[:, 3 * GROUP_W:]
    w_log = -_softplus(-(w0_ref[...] + _mm(jnp.tanh(lora), wup_ref[...]))) - RWKV_DECAY_OFFSET
    lw = -jnp.exp(w_log)
    a_gate = _sigmoid(a0_ref[...] + _mm(lora, aup_ref[...]))
    g = _mm(_sigmoid(lora), gup_ref[...])
    kk = k * kk_ref[...]
    kk = kk * lax.rsqrt(_mm(kk * kk, ones_h) + L2_EPS)
    k = k * (1.0 + (a_gate - 1.0) * ka_ref[...])
    b_v = kk * a_gate
    cl = _mm_sel_l(sel_ref[SEL_LTRI], lw)
    e_in = jnp.exp(cl)
    e_inv = jnp.exp(-cl)
    at_bf = (-kk * jnp.exp(cl - lw)).astype(BF16)
    rt_bf = (r * e_in).astype(BF16)
    bt_bf = (b_v * e_inv).astype(BF16)
    kt_bf = (k * e_inv).astype(BF16)
    v_bf = v.astype(BF16)
    same = bdb_ref[BDB_SAME]
    strict = cat_ref[CAT_STRICT]
    incl = cat_ref[CAT_INCL]
    chunks = [_chunk_rows(c) for c in range(N_CHUNKS)]
    zero = jnp.zeros((), F32)
    b_rows = [_rows(bt_bf[rs], same) for rs in chunks]
    k_rows = [_rows(kt_bf[rs], same) for rs in chunks]
    v_rows = [_rows(v_bf[rs], same) for rs in chunks]
    n_cats = [jnp.where(strict > 0, -_dg(at_bf[rs], br, 1, 1), zero).astype(BF16)
              for rs, br in zip(chunks, b_rows)]
    aks = [jnp.where(strict > 0, _dg(at_bf[rs], kr, 1, 1), zero).astype(BF16)
           for rs, kr in zip(chunks, k_rows)]
    rbs = [jnp.where(incl > 0, _dg(rt_bf[rs], br, 1, 1), zero).astype(BF16)
           for rs, br in zip(chunks, b_rows)]
    rks = [jnp.where(incl > 0, _dg(rt_bf[rs], kr, 1, 1), zero).astype(BF16)
           for rs, kr in zip(chunks, k_rows)]
    t_cats = yield n_cats
    tas = [_dg(t_cat, _rows(at_bf[rs], same), 1, 0).astype(BF16) for rs, t_cat in zip(chunks, t_cats)]
    tvs = [_dg(t_cat, _rows(_dg(ak, vr, 1, 0).astype(BF16), same), 1, 0)
           for t_cat, ak, vr in zip(t_cats, aks, v_rows)]
    y_vs = [_dg(rk, vr, 1, 0) for rk, vr in zip(rks, v_rows)]
    lasts = [cl[rs][CHUNK - 1:CHUNK, :] for rs in chunks]
    to_ends = [jnp.exp(last - cl[rs]) for rs, last in zip(chunks, lasts)]
    same_f = bdf_ref[BDF_SAME]
    b_ends = [(b_v[rs] * te).astype(BF16) for rs, te in zip(chunks, to_ends)]
    k_ends = [(k[rs] * te).astype(BF16) for rs, te in zip(chunks, to_ends)]
    m_st = [(same_f * _dg(ta, be, 0, 0)).astype(BF16) for ta, be in zip(tas, b_ends)]
    c_st = [same_f * _dg(jnp.concatenate([tv.astype(BF16), v_bf[rs]], axis=0),
                         jnp.concatenate([be, ke], axis=0), 0, 0)
            for tv, rs, be, ke in zip(tvs, chunks, b_ends, k_ends)]
    st = st_ref[...]
    sts = []
    for c in range(N_CHUNKS):
        st_bf = st.astype(BF16)
        sts.append(st_bf)
        st = st * jnp.exp(lasts[c]) + _dg(st_bf, m_st[c], 1, 0) + c_st[c]
    st_ref[...] = st
    us = [(_dg(ta, s_bf, 1, 1) + tv).astype(BF16) for ta, tv, s_bf in zip(tas, tvs, sts)]
    ys = [_dg(rt_bf[rs], s_bf, 1, 1) + _dg(rb, _rows(u, same), 1, 0) + yv
          for rs, s_bf, rb, u, yv in zip(chunks, sts, rbs, us, y_vs)]
    y = jnp.concatenate(ys, axis=0)
    inv_d = 1.0 / HEAD_DIM
    mean = _mm(y, ones_h) * inv_d
    yc = y - mean
    var = _mm(yc * yc, ones_h) * inv_d
    yn = yc * lax.rsqrt(var + RWKV_GN_EPS) * lnw_ref[...] + lnb_ref[...]
    bonus = _mm(r * k * rk_ref[...], ones_h) * v
    o_ref[0] = ((yn + bonus) * g).astype(o_ref.dtype)


SB_BLOCK = 256
SB_NEGLIGIBLE = 128.0


def _neg_abs(x):
    sign = jnp.uint32(0x80000000)
    return lax.bitcast_convert_type(lax.bitcast_convert_type(x, jnp.uint32) | sign, F32)


def _sb_kernel(q_ref, k_ref, v_ref, o_ref, acc_ref, run_ref, zz_ref, w_ref):
    qi = pl.program_id(1)
    t = SB_BLOCK
    heads = range(N_HEADS)
    scale = HEAD_DIM ** -0.5
    assert scale == 0.125
    lane_head = _iota((t, GROUP_W), 1) // HEAD_DIM
    q = q_ref[0] * jnp.asarray(scale, q_ref.dtype)
    zero = jnp.zeros((), q.dtype)
    q_h = [jnp.where(lane_head == h, q, zero) for h in heads]
    tri = (_iota((t, t), 0) >= _iota((t, t), 1)).astype(BF16)
    upper2 = jnp.concatenate([tri, tri], axis=0)
    causal = _iota((t, t), 1) < _iota((t, t), 0)
    acc_ref[...] = jnp.zeros_like(acc_ref)
    run_ref[...] = jnp.zeros_like(run_ref)

    def scores(j):
        kj = k_ref[0, pl.ds(pl.multiple_of(j * t, t), t), :]
        return [_dg(q_h[h], kj, 1, 1) for h in heads]

    def weights(zz, masked):
        parts, sums = [], []
        for h in heads:
            sp = jnp.maximum(zz[h], 0.0) + jnp.log(1.0 + jnp.exp(_neg_abs(zz[h])))
            if masked:
                sp = jnp.where(causal, sp, 0.0)
            hi, lo = _split2(sp)
            parts.append(jnp.concatenate([hi, lo], axis=1))
            sums.append(jnp.sum(sp, axis=1, keepdims=True))
        later = [_dg(p, upper2, 1, 0) for p in parts]
        wgts = []
        for h in heads:
            run = run_ref[h]
            wgt = jnp.exp(zz[h] - later[h] - jnp.concatenate([run] * (t // LANE), axis=1))
            if masked:
                wgt = jnp.where(causal, wgt, 0.0)
            wgts.append(wgt.astype(BF16))
            run_ref[h] = run + sums[h]
        return jnp.concatenate(wgts, axis=1)

    def accumulate(w_cat, j):
        vj = v_ref[0, pl.ds(pl.multiple_of(j * t, t), t), :]
        v_cat = jnp.concatenate([jnp.where(lane_head == h, vj, zero) for h in heads], axis=0)
        acc_ref[...] += _dg(w_cat, v_cat, 1, 0)

    def stash(zz, w_cat):
        for h in heads:
            zz_ref[h] = zz[h]
        w_ref[...] = w_cat

    zz_next = scores(jnp.maximum(qi - 1, 0))
    stash(zz_next, weights(scores(qi), True))

    def run_min():
        r = run_ref[0]
        for h in range(1, N_HEADS):
            r = jnp.minimum(r, run_ref[h])
        return jnp.min(r)

    def cond(carry):
        i, rmin = carry
        return jnp.logical_and(i <= qi, rmin < SB_NEGLIGIBLE)

    def body(carry):
        i, _ = carry
        j = qi - i
        accumulate(w_ref[...], j + 1)
        zz_next = scores(jnp.maximum(j - 1, 0))
        stash(zz_next, weights([zz_ref[h] for h in heads], False))
        return i + 1, run_min()

    i_end, _ = lax.while_loop(cond, body, (jnp.int32(1), run_min()))
    accumulate(w_ref[...], qi - (i_end - 1))
    o_ref[0] = acc_ref[...].astype(o_ref.dtype)


def _sb(pd):
    bsz, s, _ = pd.shape
    t = SB_BLOCK
    return pl.pallas_call(
        _sb_kernel,
        grid=(bsz, s // t),
        in_specs=[pl.BlockSpec((1, t, GROUP_W), lambda b, i: (b, i, 0)),
                  pl.BlockSpec((1, s, GROUP_W), lambda b, i: (b, 0, 1)),
                  pl.BlockSpec((1, s, GROUP_W), lambda b, i: (b, 0, 2))],
        out_specs=pl.BlockSpec((1, t, GROUP_W), lambda b, i: (b, i, 0)),
        out_shape=jax.ShapeDtypeStruct((bsz, s, GROUP_W), BF16),
        scratch_shapes=[pltpu.VMEM((t, GROUP_W), F32), pltpu.VMEM((N_HEADS, t, LANE), F32),
                        pltpu.VMEM((N_HEADS, t, t), F32), pltpu.VMEM((t, N_HEADS * t), BF16)],
        compiler_params=_params(("parallel", "arbitrary")),
        name="stickbreak",
    )(pd, pd, pd)


def _row(v, width=None):
    v = v.reshape(1, -1).astype(F32)
    if width is not None and v.shape[1] < width:
        v = jnp.pad(v, ((0, 0), (0, width - v.shape[1])))
    return v


def _pad_cols(w, width):
    return jnp.pad(w, ((0, 0), (0, width - w.shape[1])))


def _pad_rows_at(w, offset, total):
    return jnp.pad(w, ((offset, total - offset - w.shape[0]), (0, 0)))


def _mixer(h2, bsz, s, gain, w_in, ssm_conv_w, ssm_conv_b, ssm_dt_bias, ssm_a_log, ssm_d, ssm_norm,
           rwkv_mu, rwkv_w0, rwkv_w_up, rwkv_a0, rwkv_a_up, rwkv_g_up, rwkv_k_k, rwkv_k_a, rwkv_r_k,
           rwkv_ln_w, rwkv_ln_b, gdn_conv_w, gdn_a_log, gdn_dt_bias, gdn_norm, tm):
    gw = GROUP_W
    ssm_in = 4 * gw + N_HEADS
    c0 = ssm_in
    c1 = c0 + RWKV_W
    c2 = c1 + 4 * gw + 2 * N_HEADS
    wa = _pad_cols(w_in[:, :c0], SSD_W).astype(BF16)
    wb = w_in[:, c0:c1].astype(BF16)
    wc = _pad_cols(w_in[:, c1:c2], GDN_W).astype(BF16)
    wd = w_in[:, c2:].astype(BF16)
    pa, pb, pc, pd = _inproj(h2, _row(gain), [wa, wb, wc, wd], tm)

    ssd = (_ssd_body, pa.reshape(bsz, s, SSD_W),
           [ssm_conv_w.astype(F32), _row(ssm_conv_b), _row(ssm_dt_bias, LANE),
            _row(ssm_a_log, LANE), _row(jnp.repeat(ssm_d, HEAD_DIM)), _row(ssm_norm)])
    lora_w = RWKV_W - 3 * gw
    rwkv = (_rwkv_body, pb.reshape(bsz, s, RWKV_W),
            [_row(rwkv_mu), _row(rwkv_w0),
             _pad_rows_at(rwkv_w_up, 0, lora_w).astype(BF16), _row(rwkv_a0),
             _pad_rows_at(rwkv_a_up, rwkv_w_up.shape[0], lora_w).astype(BF16),
             _pad_rows_at(rwkv_g_up, rwkv_w_up.shape[0] + rwkv_a_up.shape[0], lora_w).astype(BF16),
             _row(rwkv_k_k), _row(rwkv_k_a), _row(rwkv_r_k), _row(rwkv_ln_w), _row(rwkv_ln_b)])
    pad4 = lambda t: jnp.pad(t.reshape(1, -1).astype(F32), ((0, 0), (N_HEADS, LANE - 2 * N_HEADS)))
    gdn = (_gdn_body, pc.reshape(bsz, s, GDN_W),
           [gdn_conv_w.astype(F32), pad4(gdn_a_log), pad4(gdn_dt_bias), _row(jnp.tile(gdn_norm, N_HEADS))])
    ya, yb, yc = _mixers_call("mixers", [ssd, rwkv, gdn])

    yd = _sb(pd.reshape(bsz, s, 3 * gw))

    n = bsz * s
    return [ya.reshape(n, gw), yb.reshape(n, gw), yc.reshape(n, gw), yd.reshape(n, gw)]


def kernel(x, p, ffn1_norm, ffn1_w_gate, ffn1_w_up, ffn1_w_down, mix_norm, w_in, ssm_conv_w, ssm_conv_b, ssm_dt_bias, ssm_a_log, ssm_d, ssm_norm, rwkv_mu, rwkv_w0, rwkv_w_up, rwkv_a0, rwkv_a_up, rwkv_g_up, rwkv_k_k, rwkv_k_a, rwkv_r_k, rwkv_ln_w, rwkv_ln_b, gdn_conv_w, gdn_a_log, gdn_dt_bias, gdn_norm, w_out, ffn2_norm, ffn2_w_gate, ffn2_w_up, ffn2_w_down, ple_norm, ple_w_gate, ple_w_proj, final_norm):
    bsz, s, d = x.shape
    depth = p.shape[0]
    n = bsz * s
    tm = min(512, n)
    h = x.reshape(n, d)
    p3 = p.reshape(depth, n, p.shape[-1])
    for i in range(depth):
        h = _ffn(h, _row(ffn1_norm[i]), ffn1_w_gate[i].astype(BF16), ffn1_w_up[i].astype(BF16),
                 ffn1_w_down[i].astype(BF16), tm)
        ys = _mixer(h, bsz, s, mix_norm[i], w_in[i], ssm_conv_w[i], ssm_conv_b[i], ssm_dt_bias[i],
                    ssm_a_log[i], ssm_d[i], ssm_norm[i], rwkv_mu[i], rwkv_w0[i], rwkv_w_up[i],
                    rwkv_a0[i], rwkv_a_up[i], rwkv_g_up[i], rwkv_k_k[i], rwkv_k_a[i], rwkv_r_k[i],
                    rwkv_ln_w[i], rwkv_ln_b[i], gdn_conv_w[i], gdn_a_log[i], gdn_dt_bias[i],
                    gdn_norm[i], tm)
        h = _post(h, ys, p3, i,
                  [w_out[i].astype(BF16), _row(ffn2_norm[i]), ffn2_w_gate[i].astype(BF16),
                   ffn2_w_up[i].astype(BF16), ffn2_w_down[i].astype(BF16), _row(ple_norm[i]),
                   ple_w_gate[i].astype(BF16), ple_w_proj[i].astype(BF16), _row(final_norm)],
                  i == depth - 1, tm)
    return h.reshape(bsz, s, d)
```

```python
import functools

import numpy as np
import jax
import jax.numpy as jnp
from jax import lax
from jax.experimental import pallas as pl
from jax.experimental.pallas import tpu as pltpu

F32 = jnp.float32
BF16 = jnp.bfloat16

HEAD_DIM = 64
N_HEADS = 4
GROUP_W = HEAD_DIM * N_HEADS
CHUNK = 64
N_CHUNKS = 4
CONV_K = 4
SSM_STATE = 128
NORM_EPS = 1e-6
L2_EPS = 1e-6
RWKV_GN_EPS = 64e-5
RWKV_DECAY_OFFSET = 0.5
HALO = 8
LANE = 128
VMEM_LIMIT = 56 * 1024 * 1024


def _dg(a, b, ca, cb):
    return lax.dot_general(a, b, (((ca,), (cb,)), ((), ())), preferred_element_type=F32)


_DIMS = {"nn": (1, 0), "nt": (1, 1), "tn": (0, 0)}


def _mm(a, b, kind="nn"):
    ca, cb = _DIMS[kind]
    return _dg(a.astype(BF16), b.astype(BF16), ca, cb)


def _split2(x):
    hi = x.astype(BF16)
    lo = (x - hi.astype(F32)).astype(BF16)
    return hi, lo


def _split3(x):
    hi = x.astype(BF16)
    r = x - hi.astype(F32)
    mid = r.astype(BF16)
    lo = (r - mid.astype(F32)).astype(BF16)
    return hi, mid, lo


def _mm_sel_l(sel, b, kind="nn"):
    ca, cb = _DIMS[kind]
    h, m, l = _split3(b)
    return _dg(sel, h, ca, cb) + (_dg(sel, m, ca, cb) + _dg(sel, l, ca, cb))


def _mm_sel_r(a, sel, kind="nn"):
    ca, cb = _DIMS[kind]
    h, m, l = _split3(a)
    return _dg(h, sel, ca, cb) + (_dg(m, sel, ca, cb) + _dg(l, sel, ca, cb))


def _mm_sel_r2(a, sel):
    h, l = _split2(a)
    return _dg(h, sel, 1, 0) + _dg(l, sel, 1, 0)


def _sigmoid(x):
    return 1.0 / (1.0 + jnp.exp(-x))


def _silu(x):
    return x * _sigmoid(x)


def _softplus(x):
    return jnp.maximum(x, 0.0) + jnp.log1p(jnp.exp(-jnp.abs(x)))


def _iota(shape, dim):
    return lax.broadcasted_iota(jnp.int32, shape, dim)


def _tile_rows(x, n):
    return jnp.concatenate([x] * n, axis=0)


def _rms(x, gain_row):
    ms = jnp.mean(x * x, axis=-1, keepdims=True)
    return x * lax.rsqrt(ms + NORM_EPS) * gain_row


BDF_SAME, BDF_GRP_STATE = range(2)
BDB_SAME, BDB_GRP_ROWS = range(2)
CAT_NEG, CAT_STRICT, CAT_INCL = range(3)
CATB_EYE, CATB_M8, CATB_OFF8, CATB_OFF16, CATB_OFF32 = range(5)
SEL_ONES_H, SEL_ONES_G, SEL_LTRI = range(3)


def _np_consts():
    r = np.arange(GROUP_W)[:, None]
    c = np.arange(GROUP_W)[None, :]
    same = (r // HEAD_DIM) == (c // HEAD_DIM)

    bdf = np.stack([same, (r // SSM_STATE) == ((c // HEAD_DIM) // 2)]).astype(np.float32)
    bdb = np.stack([same, ((r // HEAD_DIM) // 2) == (c // SSM_STATE)]).astype(np.float32)
    l = np.arange(CHUNK)[:, None]
    s = np.arange(GROUP_W)[None, :] % CHUNK
    cat = np.stack([np.where(l >= s, 0.0, -np.inf), l > s, l >= s]).astype(np.float32)

    def off(b):
        return ((l // (2 * b)) == (s // (2 * b))) & (((l // b) % 2) == 1) & (((s // b) % 2) == 0)

    catb = np.stack([l == s, (l // 8) == (s // 8), off(8), off(16), off(32)]).astype(np.float32)
    t = np.arange(N_CHUNKS * CHUNK)
    ltri = (t[:, None] >= t[None, :]) & ((t[:, None] // CHUNK) == (t[None, :] // CHUNK))
    sel = np.stack([same, (r // SSM_STATE) == (c // SSM_STATE), ltri]).astype(np.float32)
    er = np.arange(LANE)[:, None]
    ec = np.arange(GROUP_W)[None, :] // HEAD_DIM
    expand = np.stack([er == ec, er == ec + N_HEADS]).astype(np.float32)
    pick = np.broadcast_to(np.arange(GROUP_W)[None, :] % HEAD_DIM == 0, (CHUNK, GROUP_W)).astype(np.float32)
    return (jnp.asarray(bdf), jnp.asarray(bdb, BF16), jnp.asarray(cat), jnp.asarray(sel, BF16),
            jnp.asarray(expand, BF16), jnp.asarray(pick, BF16), jnp.asarray(catb, BF16))


def _rows(x_bf, mask_bf):
    return _tile_rows(x_bf, N_HEADS) * mask_bf


def _row_form(colvals, bdb_ref, pick_ref):
    same = bdb_ref[BDB_SAME]
    pick = pick_ref[...]
    parts = [_split3(x) for x in colvals]
    prods = [[_dg(pick, _rows(p, same), 1, 1) for p in ps] for ps in parts]
    return [h + (m + l) for h, m, l in prods]


def _inv_unit_lower(n_cats_bf, catb_ref, same):
    def mul(a_cat_bf, b_cat_bf):
        return _dg(a_cat_bf, _rows(b_cat_bf, same), 1, 0)

    eye = catb_ref[CATB_EYE]
    m8 = catb_ref[CATB_M8]
    d = [n * m8 for n in n_cats_bf]
    d2 = [mul(x, x).astype(BF16) for x in d]
    d4 = [mul(x, x).astype(BF16) for x in d2]
    t = [mul(eye - x, eye + y) for x, y in zip(d, d2)]
    t = [mul(x.astype(BF16), eye + y) for x, y in zip(t, d4)]
    for idx in (CATB_OFF8, CATB_OFF16, CATB_OFF32):
        off = catb_ref[idx]
        tb = [x.astype(BF16) for x in t]
        to = [mul(x, n * off).astype(BF16) for x, n in zip(tb, n_cats_bf)]
        t = [x - mul(y, z) for x, y, z in zip(t, to, tb)]
    return [x.astype(BF16) for x in t]


def _causal_conv(cur, halo, w_ref, first):
    halo = jnp.where(first, 0.0, halo)
    ext = jnp.concatenate([halo, cur], axis=0)
    acc = cur * w_ref[CONV_K - 1:CONV_K, :]
    for j in range(CONV_K - 1):
        sh = pltpu.roll(ext, CONV_K - 1 - j, axis=0)[HALO:]
        acc = acc + sh * w_ref[j:j + 1, :]
    return acc


def _chunk_rows(c):
    return slice(c * CHUNK, (c + 1) * CHUNK)


def _ffn_kernel(h_ref, g_ref, wg_ref, wu_ref, wd_ref, o_ref):
    x = h_ref[...]
    u = _rms(x, g_ref[...]).astype(BF16)
    a = jnp.dot(u, wg_ref[...], preferred_element_type=F32)
    b = jnp.dot(u, wu_ref[...], preferred_element_type=F32)
    act = (_silu(a) * b).astype(BF16)
    y = jnp.dot(act, wd_ref[...], preferred_element_type=F32)
    o_ref[...] = x + 0.5 * y


def _const_spec(shape):
    nd = len(shape)
    return pl.BlockSpec(shape, lambda *_: (0,) * nd, pipeline_mode=pl.Buffered(1))


def _params(sem):
    return pltpu.CompilerParams(dimension_semantics=sem, vmem_limit_bytes=VMEM_LIMIT)


def _ffn(h, gain, wg, wu, wd, tm):
    n, d = h.shape
    f = wg.shape[1]
    return pl.pallas_call(
        _ffn_kernel,
        grid=(n // tm,),
        in_specs=[pl.BlockSpec((tm, d), lambda i: (i, 0)), _const_spec((1, d)),
                  _const_spec((d, f)), _const_spec((d, f)), _const_spec((f, d))],
        out_specs=pl.BlockSpec((tm, d), lambda i: (i, 0)),
        out_shape=jax.ShapeDtypeStruct((n, d), F32),
        compiler_params=_params(("parallel",)),
        name="ffn",
    )(h, gain, wg, wu, wd)


def _inproj_kernel(h_ref, g_ref, w_ref, *refs, bounds):
    o_refs, w_scrs = refs[:len(bounds)], refs[len(bounds):]

    @pl.when(pl.program_id(0) == 0)
    def _():
        for (a, b), w_scr in zip(bounds, w_scrs):
            if w_scr.shape[1] > b - a:
                w_scr[...] = jnp.zeros_like(w_scr)
            w_scr[:, 0:b - a] = w_ref[:, a:b].astype(BF16)

    u = _rms(h_ref[...], g_ref[...]).astype(BF16)
    for w_scr, o_ref in zip(w_scrs, o_refs):
        o_ref[...] = jnp.dot(u, w_scr[...], preferred_element_type=F32).astype(o_ref.dtype)


def _inproj(h, gain, w_all, layer, bounds, widths, tm):
    n, d = h.shape
    n_in = w_all.shape[2]
    dtypes = [F32, F32, F32, BF16]
    return pl.pallas_call(
        functools.partial(_inproj_kernel, bounds=bounds),
        grid=(n // tm,),
        in_specs=[pl.BlockSpec((tm, d), lambda i: (i, 0)), _const_spec((1, d)),
                  pl.BlockSpec((None, d, n_in), lambda i: (layer, 0, 0), pipeline_mode=pl.Buffered(1))],
        out_specs=[pl.BlockSpec((tm, wd), lambda i: (i, 0)) for wd in widths],
        out_shape=[jax.ShapeDtypeStruct((n, wd), dt) for wd, dt in zip(widths, dtypes)],
        scratch_shapes=[pltpu.VMEM((d, wd), BF16) for wd in widths],
        compiler_params=_params(("arbitrary",)),
        name="inproj",
    )(h, gain, w_all)


def _post_kernel(h_ref, ya_ref, yb_ref, yc_ref, yd_ref, p_ref, wo_ref, g2_ref, wg_ref, wu_ref, wd_ref,
                 gp_ref, wpg_ref, wpp_ref, fg_ref, o_ref, *, final):
    x = h_ref[...]
    for i, y_ref in enumerate((ya_ref, yb_ref, yc_ref, yd_ref)):
        x = x + jnp.dot(y_ref[...], wo_ref[i * GROUP_W:(i + 1) * GROUP_W, :], preferred_element_type=F32)
    u = _rms(x, g2_ref[...]).astype(BF16)
    a = jnp.dot(u, wg_ref[...], preferred_element_type=F32)
    b = jnp.dot(u, wu_ref[...], preferred_element_type=F32)
    act = (_silu(a) * b).astype(BF16)
    x = x + 0.5 * jnp.dot(act, wd_ref[...], preferred_element_type=F32)
    u = _rms(x, gp_ref[...]).astype(BF16)
    gate = _sigmoid(jnp.dot(u, wpg_ref[...], preferred_element_type=F32))
    e = jnp.dot(p_ref[...].astype(BF16), wpp_ref[...], preferred_element_type=F32)
    y = x + e * gate
    if final:
        y = _rms(y, fg_ref[...])
    o_ref[...] = y


def _post(h, ys, p3, layer, consts, final, tm):
    n, d = h.shape
    pd = p3.shape[2]
    return pl.pallas_call(
        functools.partial(_post_kernel, final=final),
        grid=(n // tm,),
        in_specs=[pl.BlockSpec((tm, d), lambda i: (i, 0))]
                 + [pl.BlockSpec((tm, GROUP_W), lambda i: (i, 0)) for _ in ys]
                 + [pl.BlockSpec((None, tm, pd), lambda i: (layer, i, 0))]
                 + [_const_spec(c.shape) for c in consts],
        out_specs=pl.BlockSpec((tm, d), lambda i: (i, 0)),
        out_shape=jax.ShapeDtypeStruct((n, d), F32),
        compiler_params=_params(("parallel",)),
        name="post",
    )(h, *ys, p3, *consts)


def _halo_map(rows_per_block):
    step = rows_per_block // HALO
    return lambda b, i: (b, jnp.maximum(i * step - 1, 0), 0)


N_CONSTS = 7


def _mixers_kernel(*refs, bodies, n_params):
    k = len(bodies)
    consts = refs[2 * k:2 * k + N_CONSTS]
    pos = 2 * k + N_CONSTS
    params = []
    for n in n_params:
        params.append(refs[pos:pos + n])
        pos += n
    outs = refs[pos:pos + k]
    states = refs[pos + k:pos + 2 * k]

    @pl.when(pl.program_id(1) == 0)
    def _():
        for st_ref in states:
            st_ref[...] = jnp.zeros_like(st_ref)

    gens = [body(refs[2 * i], refs[2 * i + 1], *consts, *params[i], outs[i], states[i])
            for i, body in enumerate(bodies)]
    wanted = [next(g) for g in gens]
    flat = [n for w in wanted for n in w]
    bdb_ref, catb_ref = consts[1], consts[6]
    t_cats = _inv_unit_lower(flat, catb_ref, bdb_ref[BDB_SAME]) if flat else []
    pos = 0
    for g, w in zip(gens, wanted):
        try:
            g.send(t_cats[pos:pos + len(w)])
        except StopIteration:
            pos += len(w)
        else:
            raise AssertionError("a mixer body yields exactly once")


def _mixers_call(name, mixers):
    bsz, s, _ = mixers[0][1].shape
    t = N_CHUNKS * CHUNK
    consts = _np_consts()
    assert len(consts) == N_CONSTS
    in_specs, args = [], []
    for _, x, _ in mixers:
        w = x.shape[2]
        in_specs += [pl.BlockSpec((1, t, w), lambda b, i: (b, i, 0)), pl.BlockSpec((1, HALO, w), _halo_map(t))]
        args += [x, x]
    in_specs += [_const_spec(c.shape) for c in consts]
    args += list(consts)
    for _, _, params in mixers:
        in_specs += [_const_spec(p.shape) for p in params]
        args += list(params)
    k = len(mixers)
    return pl.pallas_call(
        functools.partial(_mixers_kernel, bodies=[m[0] for m in mixers],
                          n_params=[len(m[2]) for m in mixers]),
        grid=(bsz, s // t),
        in_specs=in_specs,
        out_specs=[pl.BlockSpec((1, t, GROUP_W), lambda b, i: (b, i, 0))] * k,
        out_shape=[jax.ShapeDtypeStruct((bsz, s, GROUP_W), BF16)] * k,
        scratch_shapes=[pltpu.VMEM((GROUP_W, GROUP_W), F32)] * k,
        compiler_params=_params(("parallel", "arbitrary")),
        name=name,
    )(*args)


SSD_W = 1152


def _ssd_body(x_ref, halo_ref, bdf_ref, bdb_ref, cat_ref, sel_ref, exp_ref, pick_ref, catb_ref,
              cw_ref, cb_ref, dtb_ref, alog_ref, dsk_ref, nw_ref, o_ref, st_ref):
    yield []
    first = pl.program_id(1) == 0
    x = x_ref[0]
    z = x[:, 0:GROUP_W]
    xbc = _causal_conv(x[:, GROUP_W:4 * GROUP_W], halo_ref[0][:, GROUP_W:4 * GROUP_W], cw_ref, first)
    xbc = _silu(xbc + cb_ref[...])
    xs = xbc[:, 0:GROUP_W]
    bm = xbc[:, GROUP_W:2 * GROUP_W].astype(BF16)
    cm = xbc[:, 2 * GROUP_W:3 * GROUP_W].astype(BF16)
    dt_pad = _softplus(x[:, 4 * GROUP_W:] + dtb_ref[...])
    la_pad = dt_pad * (-jnp.exp(alog_ref[...]))
    dt_b = _mm_sel_r(dt_pad, exp_ref[0])
    la_b = _mm_sel_r(la_pad, exp_ref[0])
    cs = _mm_sel_l(sel_ref[SEL_LTRI], la_b)
    xc = xs * dt_b
    ecs = jnp.exp(cs)
    same = bdb_ref[BDB_SAME]
    grp_rows = bdb_ref[BDB_GRP_ROWS]
    grp_state = bdf_ref[BDF_GRP_STATE]
    chunks = [_chunk_rows(c) for c in range(N_CHUNKS)]
    neg = cat_ref[CAT_NEG]
    rforms = _row_form([cs[rs] for rs in chunks], bdb_ref, pick_ref)
    segs = [jnp.exp(cs[rs] - rf + neg) for rs, rf in zip(chunks, rforms)]
    scores = [(_dg(cm[rs], _rows(bm[rs], grp_rows), 1, 1) * sg).astype(BF16)
              for rs, sg in zip(chunks, segs)]
    xc_bf = xc.astype(BF16)
    ys = [_dg(sc, _rows(xc_bf[rs], same), 1, 0) for rs, sc in zip(chunks, scores)]
    lasts = [cs[rs][CHUNK - 1:CHUNK, :] for rs in chunks]
    upds = [_dg(bm[rs], (xc[rs] * jnp.exp(last - cs[rs])).astype(BF16), 0, 0) * grp_state
            for rs, last in zip(chunks, lasts)]
    st = st_ref[...]
    for c, rs in enumerate(chunks):
        ys[c] = ys[c] + _dg(cm[rs], st.astype(BF16), 1, 0) * ecs[rs]
        st = st * jnp.exp(lasts[c]) + upds[c]
    st_ref[...] = st
    y = (jnp.concatenate(ys, axis=0) + dsk_ref[...] * xs) * _silu(z)
    ms = _mm(y * y, sel_ref[SEL_ONES_G]) * (1.0 / SSM_STATE)
    o_ref[0] = (y * lax.rsqrt(ms + NORM_EPS) * nw_ref[...]).astype(o_ref.dtype)


GDN_W = 1152


def _gdn_body(x_ref, halo_ref, bdf_ref, bdb_ref, cat_ref, sel_ref, exp_ref, pick_ref, catb_ref,
              cw_ref, alog_ref, dtb_ref, nw_ref, o_ref, st_ref):
    first = pl.program_id(1) == 0
    ones_h = sel_ref[SEL_ONES_H]
    x = x_ref[0]
    qkv = _silu(_causal_conv(x[:, 0:3 * GROUP_W], halo_ref[0][:, 0:3 * GROUP_W], cw_ref, first))
    z = x[:, 3 * GROUP_W:4 * GROUP_W]
    ba = x[:, 4 * GROUP_W:]
    q = qkv[:, 0:GROUP_W]
    k = qkv[:, GROUP_W:2 * GROUP_W]
    v = qkv[:, 2 * GROUP_W:3 * GROUP_W]
    q = q * lax.rsqrt(_mm(q * q, ones_h) + L2_EPS) * (HEAD_DIM ** -0.5)
    k = k * lax.rsqrt(_mm(k * k, ones_h) + L2_EPS)
    beta_pad = _sigmoid(ba)
    g_pad = -jnp.exp(alog_ref[...]) * _softplus(ba + dtb_ref[...])
    beta_b = _mm_sel_r2(beta_pad, exp_ref[0])
    g_b = _mm_sel_r(g_pad, exp_ref[1])
    gc = _mm_sel_l(sel_ref[SEL_LTRI], g_b)
    eg = jnp.exp(gc)
    kb = k * beta_b
    q_bf = q.astype(BF16)
    k_bf = k.astype(BF16)
    kb_bf = kb.astype(BF16)
    vb_bf = (v * beta_b).astype(BF16)
    kbg_bf = (kb * eg).astype(BF16)
    qg_bf = (q * eg).astype(BF16)
    same = bdb_ref[BDB_SAME]
    chunks = [_chunk_rows(c) for c in range(N_CHUNKS)]
    neg = cat_ref[CAT_NEG]
    strict = cat_ref[CAT_STRICT]
    rforms = _row_form([gc[rs] for rs in chunks], bdb_ref, pick_ref)
    decays = [jnp.exp(gc[rs] - rf + neg) for rs, rf in zip(chunks, rforms)]
    k_rows = [_rows(k_bf[rs], same) for rs in chunks]
    a_cats = [(_dg(kb_bf[rs], kr, 1, 1) * dc * strict).astype(BF16)
              for rs, kr, dc in zip(chunks, k_rows, decays)]
    qks = [(_dg(q_bf[rs], kr, 1, 1) * dc).astype(BF16) for rs, kr, dc in zip(chunks, k_rows, decays)]
    t_cats = yield a_cats
    uws = [_dg(t_cat, jnp.concatenate([_rows(vb_bf[rs], same), _rows(kbg_bf[rs], same)], axis=1), 1, 0)
           for rs, t_cat in zip(chunks, t_cats)]
    lasts = [gc[rs][CHUNK - 1:CHUNK, :] for rs in chunks]
    k_decs = [(k[rs] * jnp.exp(last - gc[rs])).astype(BF16) for rs, last in zip(chunks, lasts)]
    same_f = bdf_ref[BDF_SAME]
    us = [uw[:, 0:GROUP_W] for uw in uws]
    ws_bf = [uw[:, GROUP_W:].astype(BF16) for uw in uws]
    m_st = [(-same_f * _dg(kd, w, 0, 0)).astype(BF16) for kd, w in zip(k_decs, ws_bf)]
    c_st = [same_f * _dg(kd, u.astype(BF16), 0, 0) for kd, u in zip(k_decs, us)]
    st = st_ref[...]
    sts = []
    for c in range(N_CHUNKS):
        st_bf = st.astype(BF16)
        sts.append(st_bf)
        st = st * jnp.exp(lasts[c]) + _dg(m_st[c], st_bf, 1, 0) + c_st[c]
    st_ref[...] = st
    v_news = [(u - _dg(w, s_bf, 1, 0)).astype(BF16) for u, w, s_bf in zip(us, ws_bf, sts)]
    os_ = [_dg(qg_bf[rs], s_bf, 1, 0) + _dg(qk, _rows(vn, same), 1, 0)
           for rs, s_bf, qk, vn in zip(chunks, sts, qks, v_news)]
    o = jnp.concatenate(os_, axis=0)
    ms = _mm(o * o, ones_h) * (1.0 / HEAD_DIM)
    o_ref[0] = (o * lax.rsqrt(ms + NORM_EPS) * nw_ref[...] * _silu(z)).astype(o_ref.dtype)


RWKV_W = 896


def _rwkv_body(x_ref, halo_ref, bdf_ref, bdb_ref, cat_ref, sel_ref, exp_ref, pick_ref, catb_ref,
               mu_ref, w0_ref, wup_ref, a0_ref, aup_ref, gup_ref, kk_ref, ka_ref,
               rk_ref, lnw_ref, lnb_ref, o_ref, st_ref):
    first = pl.program_id(1) == 0
    ones_h = sel_ref[SEL_ONES_H]
    x = x_ref[0]
    halo = jnp.where(first, 0.0, halo_ref[0])
    shifted = pltpu.roll(jnp.concatenate([halo, x], axis=0), 1, axis=0)[HALO:]
    x = x + mu_ref[...] * (shifted - x)
    r = x[:, 0:GROUP_W]
    k = x[:, GROUP_W:2 * GROUP_W]
    v = x[:, 2 * GROUP_W:3 * GROUP_W]
    lora = x[:, 3 * GROUP_W:]
    w_log = -_softplus(-(w0_ref[...] + _mm(jnp.tanh(lora), wup_ref[...]))) - RWKV_DECAY_OFFSET
    lw = -jnp.exp(w_log)
    a_gate = _sigmoid(a0_ref[...] + _mm(lora, aup_ref[...]))
    g = _mm(_sigmoid(lora), gup_ref[...])
    kk = k * kk_ref[...]
    kk = kk * lax.rsqrt(_mm(kk * kk, ones_h) + L2_EPS)
    k = k * (1.0 + (a_gate - 1.0) * ka_ref[...])
    b_v = kk * a_gate
    cl = _mm_sel_l(sel_ref[SEL_LTRI], lw)
    e_in = jnp.exp(cl)
    e_inv = jnp.exp(-cl)
    at_bf = (-kk * jnp.exp(cl - lw)).astype(BF16)
    rt_bf = (r * e_in).astype(BF16)
    bt_bf = (b_v * e_inv).astype(BF16)
    kt_bf = (k * e_inv).astype(BF16)
    v_bf = v.astype(BF16)
    same = bdb_ref[BDB_SAME]
    strict = cat_ref[CAT_STRICT]
    incl = cat_ref[CAT_INCL]
    chunks = [_chunk_rows(c) for c in range(N_CHUNKS)]
    zero = jnp.zeros((), F32)
    b_rows = [_rows(bt_bf[rs], same) for rs in chunks]
    k_rows = [_rows(kt_bf[rs], same) for rs in chunks]
    v_rows = [_rows(v_bf[rs], same) for rs in chunks]
    n_cats = [jnp.where(strict > 0, -_dg(at_bf[rs], br, 1, 1), zero).astype(BF16)
              for rs, br in zip(chunks, b_rows)]
    aks = [jnp.where(strict > 0, _dg(at_bf[rs], kr, 1, 1), zero).astype(BF16)
           for rs, kr in zip(chunks, k_rows)]
    rbs = [jnp.where(incl > 0, _dg(rt_bf[rs], br, 1, 1), zero).astype(BF16)
           for rs, br in zip(chunks, b_rows)]
    rks = [jnp.where(incl > 0, _dg(rt_bf[rs], kr, 1, 1), zero).astype(BF16)
           for rs, kr in zip(chunks, k_rows)]
    t_cats = yield n_cats
    tas = [_dg(t_cat, _rows(at_bf[rs], same), 1, 0).astype(BF16) for rs, t_cat in zip(chunks, t_cats)]
    tvs = [_dg(t_cat, _rows(_dg(ak, vr, 1, 0).astype(BF16), same), 1, 0)
           for t_cat, ak, vr in zip(t_cats, aks, v_rows)]
    y_vs = [_dg(rk, vr, 1, 0) for rk, vr in zip(rks, v_rows)]
    lasts = [cl[rs][CHUNK - 1:CHUNK, :] for rs in chunks]
    to_ends = [jnp.exp(last - cl[rs]) for rs, last in zip(chunks, lasts)]
    same_f = bdf_ref[BDF_SAME]
    b_ends = [(b_v[rs] * te).astype(BF16) for rs, te in zip(chunks, to_ends)]
    k_ends = [(k[rs] * te).astype(BF16) for rs, te in zip(chunks, to_ends)]
    m_st = [(same_f * _dg(ta, be, 0, 0)).astype(BF16) for ta, be in zip(tas, b_ends)]
    c_st = [same_f * _dg(jnp.concatenate([tv.astype(BF16), v_bf[rs]], axis=0),
                         jnp.concatenate([be, ke], axis=0), 0, 0)
            for tv, rs, be, ke in zip(tvs, chunks, b_ends, k_ends)]
    st = st_ref[...]
    sts = []
    for c in range(N_CHUNKS):
        st_bf = st.astype(BF16)
        sts.append(st_bf)
        st = st * jnp.exp(lasts[c]) + _dg(st_bf, m_st[c], 1, 0) + c_st[c]
    st_ref[...] = st
    us = [(_dg(ta, s_bf, 1, 1) + tv).astype(BF16) for ta, tv, s_bf in zip(tas, tvs, sts)]
    ys = [_dg(rt_bf[rs], s_bf, 1, 1) + _dg(rb, _rows(u, same), 1, 0) + yv
          for rs, s_bf, rb, u, yv in zip(chunks, sts, rbs, us, y_vs)]
    y = jnp.concatenate(ys, axis=0)
    inv_d = 1.0 / HEAD_DIM
    mean = _mm(y, ones_h) * inv_d
    yc = y - mean
    var = _mm(yc * yc, ones_h) * inv_d
    yn = yc * lax.rsqrt(var + RWKV_GN_EPS) * lnw_ref[...] + lnb_ref[...]
    bonus = _mm(r * k * rk_ref[...], ones_h) * v
    o_ref[0] = ((yn + bonus) * g).astype(o_ref.dtype)


SB_BLOCK = 256
SB_NEGLIGIBLE = 128.0


def _neg_abs(x):
    sign = jnp.uint32(0x80000000)
    return lax.bitcast_convert_type(lax.bitcast_convert_type(x, jnp.uint32) | sign, F32)


def _sb_kernel(q_ref, k_ref, v_ref, o_ref, acc_ref, run_ref, zz_ref, w_ref):
    qi = pl.program_id(1)
    t = SB_BLOCK
    heads = range(N_HEADS)
    scale = HEAD_DIM ** -0.5
    assert scale == 0.125
    lane_head = _iota((t, GROUP_W), 1) // HEAD_DIM
    q = q_ref[0] * jnp.asarray(scale, q_ref.dtype)
    zero = jnp.zeros((), q.dtype)
    q_h = [jnp.where(lane_head == h, q, zero) for h in heads]
    tri = (_iota((t, t), 0) >= _iota((t, t), 1)).astype(BF16)
    upper2 = jnp.concatenate([tri, tri], axis=0)
    causal = _iota((t, t), 1) < _iota((t, t), 0)
    acc_ref[...] = jnp.zeros_like(acc_ref)
    run_ref[...] = jnp.zeros_like(run_ref)

    def scores(j):
        kj = k_ref[0, pl.ds(pl.multiple_of(j * t, t), t), :]
        return [_dg(q_h[h], kj, 1, 1) for h in heads]

    def weights(zz, masked):
        parts, sums = [], []
        for h in heads:
            sp = jnp.maximum(zz[h], 0.0) + jnp.log(1.0 + jnp.exp(_neg_abs(zz[h])))
            if masked:
                sp = jnp.where(causal, sp, 0.0)
            hi, lo = _split2(sp)
            parts.append(jnp.concatenate([hi, lo], axis=1))
            sums.append(jnp.sum(sp, axis=1, keepdims=True))
        later = [_dg(p, upper2, 1, 0) for p in parts]
        wgts = []
        for h in heads:
            run = run_ref[h]
            wgt = jnp.exp(zz[h] - later[h] - jnp.concatenate([run] * (t // LANE), axis=1))
            if masked:
                wgt = jnp.where(causal, wgt, 0.0)
            wgts.append(wgt.astype(BF16))
            run_ref[h] = run + sums[h]
        return jnp.concatenate(wgts, axis=1)

    def accumulate(w_cat, j):
        vj = v_ref[0, pl.ds(pl.multiple_of(j * t, t), t), :]
        v_cat = jnp.concatenate([jnp.where(lane_head == h, vj, zero) for h in heads], axis=0)
        acc_ref[...] += _dg(w_cat, v_cat, 1, 0)

    def stash(zz, w_cat):
        for h in heads:
            zz_ref[h] = zz[h]
        w_ref[...] = w_cat

    zz_next = scores(jnp.maximum(qi - 1, 0))
    stash(zz_next, weights(scores(qi), True))

    def run_min():
        r = run_ref[0]
        for h in range(1, N_HEADS):
            r = jnp.minimum(r, run_ref[h])
        return jnp.min(r)

    def cond(carry):
        i, rmin = carry
        return jnp.logical_and(i <= qi, rmin < SB_NEGLIGIBLE)

    def body(carry):
        i, _ = carry
        j = qi - i
        accumulate(w_ref[...], j + 1)
        zz_next = scores(jnp.maximum(j - 1, 0))
        stash(zz_next, weights([zz_ref[h] for h in heads], False))
        return i + 1, run_min()

    i_end, _ = lax.while_loop(cond, body, (jnp.int32(1), run_min()))
    accumulate(w_ref[...], qi - (i_end - 1))
    o_ref[0] = acc_ref[...].astype(o_ref.dtype)


def _sb(pd):
    bsz, s, _ = pd.shape
    t = SB_BLOCK
    return pl.pallas_call(
        _sb_kernel,
        grid=(bsz, s // t),
        in_specs=[pl.BlockSpec((1, t, GROUP_W), lambda b, i: (b, i, 0)),
                  pl.BlockSpec((1, s, GROUP_W), lambda b, i: (b, 0, 1)),
                  pl.BlockSpec((1, s, GROUP_W), lambda b, i: (b, 0, 2))],
        out_specs=pl.BlockSpec((1, t, GROUP_W), lambda b, i: (b, i, 0)),
        out_shape=jax.ShapeDtypeStruct((bsz, s, GROUP_W), BF16),
        scratch_shapes=[pltpu.VMEM((t, GROUP_W), F32), pltpu.VMEM((N_HEADS, t, LANE), F32),
                        pltpu.VMEM((N_HEADS, t, t), F32), pltpu.VMEM((t, N_HEADS * t), BF16)],
        compiler_params=_params(("parallel", "arbitrary")),
        name="stickbreak",
    )(pd, pd, pd)


def _row(v, width=None):
    v = v.reshape(1, -1).astype(F32)
    if width is not None and v.shape[1] < width:
        v = jnp.pad(v, ((0, 0), (0, width - v.shape[1])))
    return v


def _pad_rows_at(w, offset, total):
    return jnp.pad(w, ((offset, total - offset - w.shape[0]), (0, 0)))


def _mixer(h2, bsz, s, gain, w_in, layer, ssm_conv_w, ssm_conv_b, ssm_dt_bias, ssm_a_log, ssm_d, ssm_norm,
           rwkv_mu, rwkv_w0, rwkv_w_up, rwkv_a0, rwkv_a_up, rwkv_g_up, rwkv_k_k, rwkv_k_a, rwkv_r_k,
           rwkv_ln_w, rwkv_ln_b, gdn_conv_w, gdn_a_log, gdn_dt_bias, gdn_norm, tm):
    gw = GROUP_W
    ssm_in = 4 * gw + N_HEADS
    c0 = ssm_in
    c1 = c0 + RWKV_W
    c2 = c1 + 4 * gw + 2 * N_HEADS
    bounds = ((0, c0), (c0, c1), (c1, c2), (c2, w_in.shape[2]))
    pa, pb, pc, pd = _inproj(h2, _row(gain), w_in, layer, bounds, (SSD_W, RWKV_W, GDN_W, 3 * gw), tm)

    ssd = (_ssd_body, pa.reshape(bsz, s, SSD_W),
           [ssm_conv_w.astype(F32), _row(ssm_conv_b), _row(ssm_dt_bias, LANE),
            _row(ssm_a_log, LANE), _row(jnp.repeat(ssm_d, HEAD_DIM)), _row(ssm_norm)])
    lora_w = RWKV_W - 3 * gw
    rwkv = (_rwkv_body, pb.reshape(bsz, s, RWKV_W),
            [_row(rwkv_mu), _row(rwkv_w0),
             _pad_rows_at(rwkv_w_up, 0, lora_w).astype(BF16), _row(rwkv_a0),
             _pad_rows_at(rwkv_a_up, rwkv_w_up.shape[0], lora_w).astype(BF16),
             _pad_rows_at(rwkv_g_up, rwkv_w_up.shape[0] + rwkv_a_up.shape[0], lora_w).astype(BF16),
             _row(rwkv_k_k), _row(rwkv_k_a), _row(rwkv_r_k), _row(rwkv_ln_w), _row(rwkv_ln_b)])
    pad4 = lambda t: jnp.pad(t.reshape(1, -1).astype(F32), ((0, 0), (N_HEADS, LANE - 2 * N_HEADS)))
    gdn = (_gdn_body, pc.reshape(bsz, s, GDN_W),
           [gdn_conv_w.astype(F32), pad4(gdn_a_log), pad4(gdn_dt_bias), _row(jnp.tile(gdn_norm, N_HEADS))])
    ya, yb, yc = _mixers_call("mixers", [ssd, rwkv, gdn])

    yd = _sb(pd.reshape(bsz, s, 3 * gw))

    n = bsz * s
    return [ya.reshape(n, gw), yb.reshape(n, gw), yc.reshape(n, gw), yd.reshape(n, gw)]


def kernel(x, p, ffn1_norm, ffn1_w_gate, ffn1_w_up, ffn1_w_down, mix_norm, w_in, ssm_conv_w, ssm_conv_b, ssm_dt_bias, ssm_a_log, ssm_d, ssm_norm, rwkv_mu, rwkv_w0, rwkv_w_up, rwkv_a0, rwkv_a_up, rwkv_g_up, rwkv_k_k, rwkv_k_a, rwkv_r_k, rwkv_ln_w, rwkv_ln_b, gdn_conv_w, gdn_a_log, gdn_dt_bias, gdn_norm, w_out, ffn2_norm, ffn2_w_gate, ffn2_w_up, ffn2_w_down, ple_norm, ple_w_gate, ple_w_proj, final_norm):
    bsz, s, d = x.shape
    depth = p.shape[0]
    n = bsz * s
    tm = min(512, n)
    h = x.reshape(n, d)
    p3 = p.reshape(depth, n, p.shape[-1])
    for i in range(depth):
        h = _ffn(h, _row(ffn1_norm[i]), ffn1_w_gate[i].astype(BF16), ffn1_w_up[i].astype(BF16),
                 ffn1_w_down[i].astype(BF16), tm)
        ys = _mixer(h, bsz, s, mix_norm[i], w_in, i, ssm_conv_w[i], ssm_conv_b[i], ssm_dt_bias[i],
                    ssm_a_log[i], ssm_d[i], ssm_norm[i], rwkv_mu[i], rwkv_w0[i], rwkv_w_up[i],
                    rwkv_a0[i], rwkv_a_up[i], rwkv_g_up[i], rwkv_k_k[i], rwkv_k_a[i], rwkv_r_k[i],
                    rwkv_ln_w[i], rwkv_ln_b[i], gdn_conv_w[i], gdn_a_log[i], gdn_dt_bias[i],
                    gdn_norm[i], tm)
        h = _post(h, ys, p3, i,
                  [w_out[i].astype(BF16), _row(ffn2_norm[i]), ffn2_w_gate[i].astype(BF16),
                   ffn2_w_up[i].astype(BF16), ffn2_w_down[i].astype(BF16), _row(ple_norm[i]),
                   ple_w_gate[i].astype(BF16), ple_w_proj[i].astype(BF16), _row(final_norm)],
                  i == depth - 1, tm)
    return h.reshape(bsz, s, d)
```

```python
import functools

import numpy as np
import jax
import jax.numpy as jnp
from jax import lax
from jax.experimental import pallas as pl
from jax.experimental.pallas import tpu as pltpu

F32 = jnp.float32
BF16 = jnp.bfloat16

HEAD_DIM = 64
N_HEADS = 4
GROUP_W = HEAD_DIM * N_HEADS
CHUNK = 64
N_CHUNKS = 4
CONV_K = 4
SSM_STATE = 128
NORM_EPS = 1e-6
L2_EPS = 1e-6
RWKV_GN_EPS = 64e-5
RWKV_DECAY_OFFSET = 0.5
HALO = 8
LANE = 128
VMEM_LIMIT = 56 * 1024 * 1024


def _dg(a, b, ca, cb):
    return lax.dot_general(a, b, (((ca,), (cb,)), ((), ())), preferred_element_type=F32)


_DIMS = {"nn": (1, 0), "nt": (1, 1), "tn": (0, 0)}


def _mm(a, b, kind="nn"):
    ca, cb = _DIMS[kind]
    return _dg(a.astype(BF16), b.astype(BF16), ca, cb)


def _split2(x):
    hi = x.astype(BF16)
    lo = (x - hi.astype(F32)).astype(BF16)
    return hi, lo


def _split3(x):
    hi = x.astype(BF16)
    r = x - hi.astype(F32)
    mid = r.astype(BF16)
    lo = (r - mid.astype(F32)).astype(BF16)
    return hi, mid, lo


def _mm_sel_l(sel, b, kind="nn"):
    ca, cb = _DIMS[kind]
    h, m, l = _split3(b)
    return _dg(sel, h, ca, cb) + (_dg(sel, m, ca, cb) + _dg(sel, l, ca, cb))


def _mm_sel_r(a, sel, kind="nn"):
    ca, cb = _DIMS[kind]
    h, m, l = _split3(a)
    return _dg(h, sel, ca, cb) + (_dg(m, sel, ca, cb) + _dg(l, sel, ca, cb))


def _mm_sel_r2(a, sel):
    h, l = _split2(a)
    return _dg(h, sel, 1, 0) + _dg(l, sel, 1, 0)


def _sigmoid(x):
    return 1.0 / (1.0 + jnp.exp(-x))


def _silu(x):
    hx = 0.5 * x
    return hx + hx * jnp.tanh(hx)


def _softplus(x):
    return jnp.maximum(x, 0.0) + jnp.log1p(jnp.exp(-jnp.abs(x)))


def _iota(shape, dim):
    return lax.broadcasted_iota(jnp.int32, shape, dim)


def _tile_rows(x, n):
    return jnp.concatenate([x] * n, axis=0)


def _rms(x, gain_row):
    ms = jnp.mean(x * x, axis=-1, keepdims=True)
    return x * lax.rsqrt(ms + NORM_EPS) * gain_row


BDF_SAME, BDF_GRP_STATE = range(2)
BDB_SAME, BDB_GRP_ROWS = range(2)
CAT_NEG, CAT_STRICT, CAT_INCL = range(3)
CATB_EYE, CATB_M8, CATB_OFF8, CATB_OFF16, CATB_OFF32 = range(5)
SEL_ONES_H, SEL_ONES_G, SEL_LTRI = range(3)


def _np_consts():
    r = np.arange(GROUP_W)[:, None]
    c = np.arange(GROUP_W)[None, :]
    same = (r // HEAD_DIM) == (c // HEAD_DIM)

    bdf = np.stack([same, (r // SSM_STATE) == ((c // HEAD_DIM) // 2)]).astype(np.float32)
    bdb = np.stack([same, ((r // HEAD_DIM) // 2) == (c // SSM_STATE)]).astype(np.float32)
    l = np.arange(CHUNK)[:, None]
    s = np.arange(GROUP_W)[None, :] % CHUNK
    cat = np.stack([np.where(l >= s, 0.0, -np.inf), l > s, l >= s]).astype(np.float32)

    def off(b):
        return ((l // (2 * b)) == (s // (2 * b))) & (((l // b) % 2) == 1) & (((s // b) % 2) == 0)

    catb = np.stack([l == s, (l // 8) == (s // 8), off(8), off(16), off(32)]).astype(np.float32)
    t = np.arange(N_CHUNKS * CHUNK)
    ltri = (t[:, None] >= t[None, :]) & ((t[:, None] // CHUNK) == (t[None, :] // CHUNK))
    sel = np.stack([same, (r // SSM_STATE) == (c // SSM_STATE), ltri]).astype(np.float32)
    er = np.arange(LANE)[:, None]
    ec = np.arange(GROUP_W)[None, :] // HEAD_DIM
    expand = np.stack([er == ec, er == ec + N_HEADS]).astype(np.float32)
    pick = np.broadcast_to(np.arange(GROUP_W)[None, :] % HEAD_DIM == 0, (CHUNK, GROUP_W)).astype(np.float32)
    return (jnp.asarray(bdf), jnp.asarray(bdb, BF16), jnp.asarray(cat), jnp.asarray(sel, BF16),
            jnp.asarray(expand, BF16), jnp.asarray(pick, BF16), jnp.asarray(catb, BF16))


def _rows(x_bf, mask_bf):
    return _tile_rows(x_bf, N_HEADS) * mask_bf


def _row_form(colvals, bdb_ref, pick_ref):
    same = bdb_ref[BDB_SAME]
    pick = pick_ref[...]
    parts = [_split3(x) for x in colvals]
    prods = [[_dg(pick, _rows(p, same), 1, 1) for p in ps] for ps in parts]
    return [h + (m + l) for h, m, l in prods]


def _inv_unit_lower(n_cats_bf, catb_ref, same):
    def mul(a_cat_bf, b_cat_bf):
        return _dg(a_cat_bf, _rows(b_cat_bf, same), 1, 0)

    eye = catb_ref[CATB_EYE]
    m8 = catb_ref[CATB_M8]
    d = [n * m8 for n in n_cats_bf]
    d2 = [mul(x, x).astype(BF16) for x in d]
    d4 = [mul(x, x).astype(BF16) for x in d2]
    t = [mul(eye - x, eye + y) for x, y in zip(d, d2)]
    t = [mul(x.astype(BF16), eye + y) for x, y in zip(t, d4)]
    for idx in (CATB_OFF8, CATB_OFF16, CATB_OFF32):
        off = catb_ref[idx]
        tb = [x.astype(BF16) for x in t]
        to = [mul(x, n * off).astype(BF16) for x, n in zip(tb, n_cats_bf)]
        t = [x - mul(y, z) for x, y, z in zip(t, to, tb)]
    return [x.astype(BF16) for x in t]


def _causal_conv(cur, halo, w_ref, first):
    halo = jnp.where(first, 0.0, halo)
    ext = jnp.concatenate([halo, cur], axis=0)
    acc = cur * w_ref[CONV_K - 1:CONV_K, :]
    for j in range(CONV_K - 1):
        sh = pltpu.roll(ext, CONV_K - 1 - j, axis=0)[HALO:]
        acc = acc + sh * w_ref[j:j + 1, :]
    return acc


def _chunk_rows(c):
    return slice(c * CHUNK, (c + 1) * CHUNK)


def _ffn_kernel(h_ref, g_ref, wg_ref, wu_ref, wd_ref, o_ref):
    x = h_ref[...]
    u = _rms(x, g_ref[...]).astype(BF16)
    a = jnp.dot(u, wg_ref[...], preferred_element_type=F32)
    b = jnp.dot(u, wu_ref[...], preferred_element_type=F32)
    act = (_silu(a) * b).astype(BF16)
    y = jnp.dot(act, wd_ref[...], preferred_element_type=F32)
    o_ref[...] = x + 0.5 * y


def _const_spec(shape):
    nd = len(shape)
    return pl.BlockSpec(shape, lambda *_: (0,) * nd, pipeline_mode=pl.Buffered(1))


def _params(sem):
    return pltpu.CompilerParams(dimension_semantics=sem, vmem_limit_bytes=VMEM_LIMIT)


def _ffn(h, gain, wg, wu, wd, tm):
    n, d = h.shape
    f = wg.shape[1]
    return pl.pallas_call(
        _ffn_kernel,
        grid=(n // tm,),
        in_specs=[pl.BlockSpec((tm, d), lambda i: (i, 0)), _const_spec((1, d)),
                  _const_spec((d, f)), _const_spec((d, f)), _const_spec((f, d))],
        out_specs=pl.BlockSpec((tm, d), lambda i: (i, 0)),
        out_shape=jax.ShapeDtypeStruct((n, d), F32),
        compiler_params=_params(("parallel",)),
        name="ffn",
    )(h, gain, wg, wu, wd)


def _inproj_kernel(h_ref, g_ref, w_ref, *refs, bounds):
    o_refs, w_scrs = refs[:len(bounds)], refs[len(bounds):]

    @pl.when(pl.program_id(0) == 0)
    def _():
        for (a, b), w_scr in zip(bounds, w_scrs):
            if w_scr.shape[1] > b - a:
                w_scr[...] = jnp.zeros_like(w_scr)
            w_scr[:, 0:b - a] = w_ref[:, a:b].astype(BF16)

    u = _rms(h_ref[...], g_ref[...]).astype(BF16)
    for w_scr, o_ref in zip(w_scrs, o_refs):
        o_ref[...] = jnp.dot(u, w_scr[...], preferred_element_type=F32).astype(o_ref.dtype)


def _inproj(h, gain, w_all, layer, bounds, widths, tm):
    n, d = h.shape
    n_in = w_all.shape[2]
    dtypes = [F32, F32, F32, BF16]
    return pl.pallas_call(
        functools.partial(_inproj_kernel, bounds=bounds),
        grid=(n // tm,),
        in_specs=[pl.BlockSpec((tm, d), lambda i: (i, 0)), _const_spec((1, d)),
                  pl.BlockSpec((None, d, n_in), lambda i: (layer, 0, 0), pipeline_mode=pl.Buffered(1))],
        out_specs=[pl.BlockSpec((tm, wd), lambda i: (i, 0)) for wd in widths],
        out_shape=[jax.ShapeDtypeStruct((n, wd), dt) for wd, dt in zip(widths, dtypes)],
        scratch_shapes=[pltpu.VMEM((d, wd), BF16) for wd in widths],
        compiler_params=_params(("arbitrary",)),
        name="inproj",
    )(h, gain, w_all)


def _post_kernel(h_ref, ya_ref, yb_ref, yc_ref, yd_ref, p_ref, wo_ref, g2_ref, wg_ref, wu_ref, wd_ref,
                 gp_ref, wpg_ref, wpp_ref, fg_ref, o_ref, *, final):
    x = h_ref[...]
    for i, y_ref in enumerate((ya_ref, yb_ref, yc_ref, yd_ref)):
        x = x + jnp.dot(y_ref[...], wo_ref[i * GROUP_W:(i + 1) * GROUP_W, :], preferred_element_type=F32)
    u = _rms(x, g2_ref[...]).astype(BF16)
    a = jnp.dot(u, wg_ref[...], preferred_element_type=F32)
    b = jnp.dot(u, wu_ref[...], preferred_element_type=F32)
    act = (_silu(a) * b).astype(BF16)
    x = x + 0.5 * jnp.dot(act, wd_ref[...], preferred_element_type=F32)
    u = _rms(x, gp_ref[...]).astype(BF16)
    gate = _sigmoid(jnp.dot(u, wpg_ref[...], preferred_element_type=F32))
    e = jnp.dot(p_ref[...].astype(BF16), wpp_ref[...], preferred_element_type=F32)
    y = x + e * gate
    if final:
        y = _rms(y, fg_ref[...])
    o_ref[...] = y


def _post(h, ys, p3, layer, consts, final, tm):
    n, d = h.shape
    pd = p3.shape[2]
    return pl.pallas_call(
        functools.partial(_post_kernel, final=final),
        grid=(n // tm,),
        in_specs=[pl.BlockSpec((tm, d), lambda i: (i, 0))]
                 + [pl.BlockSpec((tm, GROUP_W), lambda i: (i, 0)) for _ in ys]
                 + [pl.BlockSpec((None, tm, pd), lambda i: (layer, i, 0))]
                 + [_const_spec(c.shape) for c in consts],
        out_specs=pl.BlockSpec((tm, d), lambda i: (i, 0)),
        out_shape=jax.ShapeDtypeStruct((n, d), F32),
        compiler_params=_params(("parallel",)),
        name="post",
    )(h, *ys, p3, *consts)


def _halo_map(rows_per_block):
    step = rows_per_block // HALO
    return lambda b, i: (b, jnp.maximum(i * step - 1, 0), 0)


N_CONSTS = 7


def _mixers_kernel(*refs, bodies, n_params):
    k = len(bodies)
    consts = refs[2 * k:2 * k + N_CONSTS]
    pos = 2 * k + N_CONSTS
    params = []
    for n in n_params:
        params.append(refs[pos:pos + n])
        pos += n
    outs = refs[pos:pos + k]
    states = refs[pos + k:pos + 2 * k]

    @pl.when(pl.program_id(1) == 0)
    def _():
        for st_ref in states:
            st_ref[...] = jnp.zeros_like(st_ref)

    gens = [body(refs[2 * i], refs[2 * i + 1], *consts, *params[i], outs[i], states[i])
            for i, body in enumerate(bodies)]
    wanted = [next(g) for g in gens]
    flat = [n for w in wanted for n in w]
    bdb_ref, catb_ref = consts[1], consts[6]
    t_cats = _inv_unit_lower(flat, catb_ref, bdb_ref[BDB_SAME]) if flat else []
    pos = 0
    for g, w in zip(gens, wanted):
        try:
            g.send(t_cats[pos:pos + len(w)])
        except StopIteration:
            pos += len(w)
        else:
            raise AssertionError("a mixer body yields exactly once")


def _mixers_call(name, mixers):
    bsz, s, _ = mixers[0][1].shape
    t = N_CHUNKS * CHUNK
    consts = _np_consts()
    assert len(consts) == N_CONSTS
    in_specs, args = [], []
    for _, x, _ in mixers:
        w = x.shape[2]
        in_specs += [pl.BlockSpec((1, t, w), lambda b, i: (b, i, 0)), pl.BlockSpec((1, HALO, w), _halo_map(t))]
        args += [x, x]
    in_specs += [_const_spec(c.shape) for c in consts]
    args += list(consts)
    for _, _, params in mixers:
        in_specs += [_const_spec(p.shape) for p in params]
        args += list(params)
    k = len(mixers)
    return pl.pallas_call(
        functools.partial(_mixers_kernel, bodies=[m[0] for m in mixers],
                          n_params=[len(m[2]) for m in mixers]),
        grid=(bsz, s // t),
        in_specs=in_specs,
        out_specs=[pl.BlockSpec((1, t, GROUP_W), lambda b, i: (b, i, 0))] * k,
        out_shape=[jax.ShapeDtypeStruct((bsz, s, GROUP_W), BF16)] * k,
        scratch_shapes=[pltpu.VMEM((GROUP_W, GROUP_W), F32)] * k,
        compiler_params=_params(("parallel", "arbitrary")),
        name=name,
    )(*args)


SSD_W = 1152


def _ssd_body(x_ref, halo_ref, bdf_ref, bdb_ref, cat_ref, sel_ref, exp_ref, pick_ref, catb_ref,
              cw_ref, cb_ref, dtb_ref, alog_ref, dsk_ref, nw_ref, o_ref, st_ref):
    yield []
    first = pl.program_id(1) == 0
    x = x_ref[0]
    z = x[:, 0:GROUP_W]
    xbc = _causal_conv(x[:, GROUP_W:4 * GROUP_W], halo_ref[0][:, GROUP_W:4 * GROUP_W], cw_ref, first)
    xbc = _silu(xbc + cb_ref[...])
    xs = xbc[:, 0:GROUP_W]
    bm = xbc[:, GROUP_W:2 * GROUP_W].astype(BF16)
    cm = xbc[:, 2 * GROUP_W:3 * GROUP_W].astype(BF16)
    dt_pad = _softplus(x[:, 4 * GROUP_W:] + dtb_ref[...])
    la_pad = dt_pad * (-jnp.exp(alog_ref[...]))
    dt_b = _mm_sel_r(dt_pad, exp_ref[0])
    la_b = _mm_sel_r(la_pad, exp_ref[0])
    cs = _mm_sel_l(sel_ref[SEL_LTRI], la_b)
    xc = xs * dt_b
    ecs = jnp.exp(cs)
    same = bdb_ref[BDB_SAME]
    grp_rows = bdb_ref[BDB_GRP_ROWS]
    grp_state = bdf_ref[BDF_GRP_STATE]
    chunks = [_chunk_rows(c) for c in range(N_CHUNKS)]
    neg = cat_ref[CAT_NEG]
    rforms = _row_form([cs[rs] for rs in chunks], bdb_ref, pick_ref)
    segs = [jnp.exp(cs[rs] - rf + neg) for rs, rf in zip(chunks, rforms)]
    scores = [(_dg(cm[rs], _rows(bm[rs], grp_rows), 1, 1) * sg).astype(BF16)
              for rs, sg in zip(chunks, segs)]
    xc_bf = xc.astype(BF16)
    ys = [_dg(sc, _rows(xc_bf[rs], same), 1, 0) for rs, sc in zip(chunks, scores)]
    lasts = [cs[rs][CHUNK - 1:CHUNK, :] for rs in chunks]
    upds = [_dg(bm[rs], (xc[rs] * jnp.exp(last - cs[rs])).astype(BF16), 0, 0) * grp_state
            for rs, last in zip(chunks, lasts)]
    st = st_ref[...]
    for c, rs in enumerate(chunks):
        ys[c] = ys[c] + _dg(cm[rs], st.astype(BF16), 1, 0) * ecs[rs]
        st = st * jnp.exp(lasts[c]) + upds[c]
    st_ref[...] = st
    y = (jnp.concatenate(ys, axis=0) + dsk_ref[...] * xs) * _silu(z)
    ms = _mm(y * y, sel_ref[SEL_ONES_G]) * (1.0 / SSM_STATE)
    o_ref[0] = (y * lax.rsqrt(ms + NORM_EPS) * nw_ref[...]).astype(o_ref.dtype)


GDN_W = 1152


def _gdn_body(x_ref, halo_ref, bdf_ref, bdb_ref, cat_ref, sel_ref, exp_ref, pick_ref, catb_ref,
              cw_ref, alog_ref, dtb_ref, nw_ref, o_ref, st_ref):
    first = pl.program_id(1) == 0
    ones_h = sel_ref[SEL_ONES_H]
    x = x_ref[0]
    qkv = _silu(_causal_conv(x[:, 0:3 * GROUP_W], halo_ref[0][:, 0:3 * GROUP_W], cw_ref, first))
    z = x[:, 3 * GROUP_W:4 * GROUP_W]
    ba = x[:, 4 * GROUP_W:]
    q = qkv[:, 0:GROUP_W]
    k = qkv[:, GROUP_W:2 * GROUP_W]
    v = qkv[:, 2 * GROUP_W:3 * GROUP_W]
    q = q * lax.rsqrt(_mm(q * q, ones_h) + L2_EPS) * (HEAD_DIM ** -0.5)
    k = k * lax.rsqrt(_mm(k * k, ones_h) + L2_EPS)
    beta_pad = _sigmoid(ba)
    g_pad = -jnp.exp(alog_ref[...]) * _softplus(ba + dtb_ref[...])
    beta_b = _mm_sel_r2(beta_pad, exp_ref[0])
    g_b = _mm_sel_r(g_pad, exp_ref[1])
    gc = _mm_sel_l(sel_ref[SEL_LTRI], g_b)
    eg = jnp.exp(gc)
    kb = k * beta_b
    q_bf = q.astype(BF16)
    k_bf = k.astype(BF16)
    kb_bf = kb.astype(BF16)
    vb_bf = (v * beta_b).astype(BF16)
    kbg_bf = (kb * eg).astype(BF16)
    qg_bf = (q * eg).astype(BF16)
    same = bdb_ref[BDB_SAME]
    chunks = [_chunk_rows(c) for c in range(N_CHUNKS)]
    neg = cat_ref[CAT_NEG]
    strict = cat_ref[CAT_STRICT]
    rforms = _row_form([gc[rs] for rs in chunks], bdb_ref, pick_ref)
    decays = [jnp.exp(gc[rs] - rf + neg) for rs, rf in zip(chunks, rforms)]
    k_rows = [_rows(k_bf[rs], same) for rs in chunks]
    a_cats = [(_dg(kb_bf[rs], kr, 1, 1) * dc * strict).astype(BF16)
              for rs, kr, dc in zip(chunks, k_rows, decays)]
    qks = [(_dg(q_bf[rs], kr, 1, 1) * dc).astype(BF16) for rs, kr, dc in zip(chunks, k_rows, decays)]
    t_cats = yield a_cats
    uws = [_dg(t_cat, jnp.concatenate([_rows(vb_bf[rs], same), _rows(kbg_bf[rs], same)], axis=1), 1, 0)
           for rs, t_cat in zip(chunks, t_cats)]
    lasts = [gc[rs][CHUNK - 1:CHUNK, :] for rs in chunks]
    k_decs = [(k[rs] * jnp.exp(last - gc[rs])).astype(BF16) for rs, last in zip(chunks, lasts)]
    same_f = bdf_ref[BDF_SAME]
    us = [uw[:, 0:GROUP_W] for uw in uws]
    ws_bf = [uw[:, GROUP_W:].astype(BF16) for uw in uws]
    m_st = [(-same_f * _dg(kd, w, 0, 0)).astype(BF16) for kd, w in zip(k_decs, ws_bf)]
    c_st = [same_f * _dg(kd, u.astype(BF16), 0, 0) for kd, u in zip(k_decs, us)]
    st = st_ref[...]
    sts = []
    for c in range(N_CHUNKS):
        st_bf = st.astype(BF16)
        sts.append(st_bf)
        st = st * jnp.exp(lasts[c]) + _dg(m_st[c], st_bf, 1, 0) + c_st[c]
    st_ref[...] = st
    v_news = [(u - _dg(w, s_bf, 1, 0)).astype(BF16) for u, w, s_bf in zip(us, ws_bf, sts)]
    os_ = [_dg(qg_bf[rs], s_bf, 1, 0) + _dg(qk, _rows(vn, same), 1, 0)
           for rs, s_bf, qk, vn in zip(chunks, sts, qks, v_news)]
    o = jnp.concatenate(os_, axis=0)
    ms = _mm(o * o, ones_h) * (1.0 / HEAD_DIM)
    o_ref[0] = (o * lax.rsqrt(ms + NORM_EPS) * nw_ref[...] * _silu(z)).astype(o_ref.dtype)


RWKV_W = 896


def _rwkv_body(x_ref, halo_ref, bdf_ref, bdb_ref, cat_ref, sel_ref, exp_ref, pick_ref, catb_ref,
               mu_ref, w0_ref, wup_ref, a0_ref, aup_ref, gup_ref, kk_ref, ka_ref,
               rk_ref, lnw_ref, lnb_ref, o_ref, st_ref):
    first = pl.program_id(1) == 0
    ones_h = sel_ref[SEL_ONES_H]
    x = x_ref[0]
    halo = jnp.where(first, 0.0, halo_ref[0])
    shifted = pltpu.roll(jnp.concatenate([halo, x], axis=0), 1, axis=0)[HALO:]
    x = x + mu_ref[...] * (shifted - x)
    r = x[:, 0:GROUP_W]
    k = x[:, GROUP_W:2 * GROUP_W]
    v = x[:, 2 * GROUP_W:3 * GROUP_W]
    lora = x[:, 3 * GROUP_W:]
    w_log = -_softplus(-(w0_ref[...] + _mm(jnp.tanh(lora), wup_ref[...]))) - RWKV_DECAY_OFFSET
    lw = -jnp.exp(w_log)
    a_gate = _sigmoid(a0_ref[...] + _mm(lora, aup_ref[...]))
    g = _mm(_sigmoid(lora), gup_ref[...])
    kk = k * kk_ref[...]
    kk = kk * lax.rsqrt(_mm(kk * kk, ones_h) + L2_EPS)
    k = k * (1.0 + (a_gate - 1.0) * ka_ref[...])
    b_v = kk * a_gate
    cl = _mm_sel_l(sel_ref[SEL_LTRI], lw)
    e_in = jnp.exp(cl)
    e_inv = jnp.exp(-cl)
    at_bf = (-kk * jnp.exp(cl - lw)).astype(BF16)
    rt_bf = (r * e_in).astype(BF16)
    bt_bf = (b_v * e_inv).astype(BF16)
    kt_bf = (k * e_inv).astype(BF16)
    v_bf = v.astype(BF16)
    same = bdb_ref[BDB_SAME]
    strict = cat_ref[CAT_STRICT]
    incl = cat_ref[CAT_INCL]
    chunks = [_chunk_rows(c) for c in range(N_CHUNKS)]
    zero = jnp.zeros((), F32)
    b_rows = [_rows(bt_bf[rs], same) for rs in chunks]
    k_rows = [_rows(kt_bf[rs], same) for rs in chunks]
    v_rows = [_rows(v_bf[rs], same) for rs in chunks]
    n_cats = [jnp.where(strict > 0, -_dg(at_bf[rs], br, 1, 1), zero).astype(BF16)
              for rs, br in zip(chunks, b_rows)]
    aks = [jnp.where(strict > 0, _dg(at_bf[rs], kr, 1, 1), zero).astype(BF16)
           for rs, kr in zip(chunks, k_rows)]
    rbs = [jnp.where(incl > 0, _dg(rt_bf[rs], br, 1, 1), zero).astype(BF16)
           for rs, br in zip(chunks, b_rows)]
    rks = [jnp.where(incl > 0, _dg(rt_bf[rs], kr, 1, 1), zero).astype(BF16)
           for rs, kr in zip(chunks, k_rows)]
    t_cats = yield n_cats
    tas = [_dg(t_cat, _rows(at_bf[rs], same), 1, 0).astype(BF16) for rs, t_cat in zip(chunks, t_cats)]
    tvs = [_dg(t_cat, _rows(_dg(ak, vr, 1, 0).astype(BF16), same), 1, 0)
           for t_cat, ak, vr in zip(t_cats, aks, v_rows)]
    y_vs = [_dg(rk, vr, 1, 0) for rk, vr in zip(rks, v_rows)]
    lasts = [cl[rs][CHUNK - 1:CHUNK, :] for rs in chunks]
    to_ends = [jnp.exp(last - cl[rs]) for rs, last in zip(chunks, lasts)]
    same_f = bdf_ref[BDF_SAME]
    b_ends = [(b_v[rs] * te).astype(BF16) for rs, te in zip(chunks, to_ends)]
    k_ends = [(k[rs] * te).astype(BF16) for rs, te in zip(chunks, to_ends)]
    m_st = [(same_f * _dg(ta, be, 0, 0)).astype(BF16) for ta, be in zip(tas, b_ends)]
    c_st = [same_f * _dg(jnp.concatenate([tv.astype(BF16), v_bf[rs]], axis=0),
                         jnp.concatenate([be, ke], axis=0), 0, 0)
            for tv, rs, be, ke in zip(tvs, chunks, b_ends, k_ends)]
    st = st_ref[...]
    sts = []
    for c in range(N_CHUNKS):
        st_bf = st.astype(BF16)
        sts.append(st_bf)
        st = st * jnp.exp(lasts[c]) + _dg(st_bf, m_st[c], 1, 0) + c_st[c]
    st_ref[...] = st
    us = [(_dg(ta, s_bf, 1, 1) + tv).astype(BF16) for ta, tv, s_bf in zip(tas, tvs, sts)]
    ys = [_dg(rt_bf[rs], s_bf, 1, 1) + _dg(rb, _rows(u, same), 1, 0) + yv
          for rs, s_bf, rb, u, yv in zip(chunks, sts, rbs, us, y_vs)]
    y = jnp.concatenate(ys, axis=0)
    inv_d = 1.0 / HEAD_DIM
    mean = _mm(y, ones_h) * inv_d
    yc = y - mean
    var = _mm(yc * yc, ones_h) * inv_d
    yn = yc * lax.rsqrt(var + RWKV_GN_EPS) * lnw_ref[...] + lnb_ref[...]
    bonus = _mm(r * k * rk_ref[...], ones_h) * v
    o_ref[0] = ((yn + bonus) * g).astype(o_ref.dtype)


SB_BLOCK = 256
SB_NEGLIGIBLE = 128.0


def _neg_abs(x):
    sign = jnp.uint32(0x80000000)
    return lax.bitcast_convert_type(lax.bitcast_convert_type(x, jnp.uint32) | sign, F32)


def _sb_kernel(q_ref, k_ref, v_ref, o_ref, acc_ref, run_ref, zz_ref, w_ref):
    qi = pl.program_id(1)
    t = SB_BLOCK
    heads = range(N_HEADS)
    scale = HEAD_DIM ** -0.5
    assert scale == 0.125
    lane_head = _iota((t, GROUP_W), 1) // HEAD_DIM
    q = q_ref[0] * jnp.asarray(scale, q_ref.dtype)
    zero = jnp.zeros((), q.dtype)
    q_h = [jnp.where(lane_head == h, q, zero) for h in heads]
    tri = (_iota((t, t), 0) >= _iota((t, t), 1)).astype(BF16)
    upper2 = jnp.concatenate([tri, tri], axis=0)
    causal = _iota((t, t), 1) < _iota((t, t), 0)
    acc_ref[...] = jnp.zeros_like(acc_ref)
    run_ref[...] = jnp.zeros_like(run_ref)

    def scores(j):
        kj = k_ref[0, pl.ds(pl.multiple_of(j * t, t), t), :]
        return [_dg(q_h[h], kj, 1, 1) for h in heads]

    def weights(zz, masked):
        parts, sums = [], []
        for h in heads:
            sp = jnp.maximum(zz[h], 0.0) + jnp.log(1.0 + jnp.exp(_neg_abs(zz[h])))
            if masked:
                sp = jnp.where(causal, sp, 0.0)
            hi, lo = _split2(sp)
            parts.append(jnp.concatenate([hi, lo], axis=1))
            sums.append(jnp.sum(sp, axis=1, keepdims=True))
        later = [_dg(p, upper2, 1, 0) for p in parts]
        wgts = []
        for h in heads:
            run = run_ref[h]
            wgt = jnp.exp(zz[h] - later[h] - jnp.concatenate([run] * (t // LANE), axis=1))
            if masked:
                wgt = jnp.where(causal, wgt, 0.0)
            wgts.append(wgt.astype(BF16))
            run_ref[h] = run + sums[h]
        return jnp.concatenate(wgts, axis=1)

    def accumulate(w_cat, j):
        vj = v_ref[0, pl.ds(pl.multiple_of(j * t, t), t), :]
        v_cat = jnp.concatenate([jnp.where(lane_head == h, vj, zero) for h in heads], axis=0)
        acc_ref[...] += _dg(w_cat, v_cat, 1, 0)

    def stash(zz, w_cat):
        for h in heads:
            zz_ref[h] = zz[h]
        w_ref[...] = w_cat

    zz_next = scores(jnp.maximum(qi - 1, 0))
    stash(zz_next, weights(scores(qi), True))

    def run_min():
        r = run_ref[0]
        for h in range(1, N_HEADS):
            r = jnp.minimum(r, run_ref[h])
        return jnp.min(r)

    def cond(carry):
        i, rmin = carry
        return jnp.logical_and(i <= qi, rmin < SB_NEGLIGIBLE)

    def body(carry):
        i, _ = carry
        j = qi - i
        accumulate(w_ref[...], j + 1)
        zz_next = scores(jnp.maximum(j - 1, 0))
        stash(zz_next, weights([zz_ref[h] for h in heads], False))
        return i + 1, run_min()

    i_end, _ = lax.while_loop(cond, body, (jnp.int32(1), run_min()))
    accumulate(w_ref[...], qi - (i_end - 1))
    o_ref[0] = acc_ref[...].astype(o_ref.dtype)


def _sb(pd):
    bsz, s, _ = pd.shape
    t = SB_BLOCK
    return pl.pallas_call(
        _sb_kernel,
        grid=(bsz, s // t),
        in_specs=[pl.BlockSpec((1, t, GROUP_W), lambda b, i: (b, i, 0)),
                  pl.BlockSpec((1, s, GROUP_W), lambda b, i: (b, 0, 1)),
                  pl.BlockSpec((1, s, GROUP_W), lambda b, i: (b, 0, 2))],
        out_specs=pl.BlockSpec((1, t, GROUP_W), lambda b, i: (b, i, 0)),
        out_shape=jax.ShapeDtypeStruct((bsz, s, GROUP_W), BF16),
        scratch_shapes=[pltpu.VMEM((t, GROUP_W), F32), pltpu.VMEM((N_HEADS, t, LANE), F32),
                        pltpu.VMEM((N_HEADS, t, t), F32), pltpu.VMEM((t, N_HEADS * t), BF16)],
        compiler_params=_params(("parallel", "arbitrary")),
        name="stickbreak",
    )(pd, pd, pd)


def _row(v, width=None):
    v = v.reshape(1, -1).astype(F32)
    if width is not None and v.shape[1] < width:
        v = jnp.pad(v, ((0, 0), (0, width - v.shape[1])))
    return v


def _pad_rows_at(w, offset, total):
    return jnp.pad(w, ((offset, total - offset - w.shape[0]), (0, 0)))


def _mixer(h2, bsz, s, gain, w_in, layer, ssm_conv_w, ssm_conv_b, ssm_dt_bias, ssm_a_log, ssm_d, ssm_norm,
           rwkv_mu, rwkv_w0, rwkv_w_up, rwkv_a0, rwkv_a_up, rwkv_g_up, rwkv_k_k, rwkv_k_a, rwkv_r_k,
           rwkv_ln_w, rwkv_ln_b, gdn_conv_w, gdn_a_log, gdn_dt_bias, gdn_norm, tm):
    gw = GROUP_W
    ssm_in = 4 * gw + N_HEADS
    c0 = ssm_in
    c1 = c0 + RWKV_W
    c2 = c1 + 4 * gw + 2 * N_HEADS
    bounds = ((0, c0), (c0, c1), (c1, c2), (c2, w_in.shape[2]))
    pa, pb, pc, pd = _inproj(h2, _row(gain), w_in, layer, bounds, (SSD_W, RWKV_W, GDN_W, 3 * gw), tm)

    ssd = (_ssd_body, pa.reshape(bsz, s, SSD_W),
           [ssm_conv_w.astype(F32), _row(ssm_conv_b), _row(ssm_dt_bias, LANE),
            _row(ssm_a_log, LANE), _row(jnp.repeat(ssm_d, HEAD_DIM)), _row(ssm_norm)])
    lora_w = RWKV_W - 3 * gw
    rwkv = (_rwkv_body, pb.reshape(bsz, s, RWKV_W),
            [_row(rwkv_mu), _row(rwkv_w0),
             _pad_rows_at(rwkv_w_up, 0, lora_w).astype(BF16), _row(rwkv_a0),
             _pad_rows_at(rwkv_a_up, rwkv_w_up.shape[0], lora_w).astype(BF16),
             _pad_rows_at(rwkv_g_up, rwkv_w_up.shape[0] + rwkv_a_up.shape[0], lora_w).astype(BF16),
             _row(rwkv_k_k), _row(rwkv_k_a), _row(rwkv_r_k), _row(rwkv_ln_w), _row(rwkv_ln_b)])
    pad4 = lambda t: jnp.pad(t.reshape(1, -1).astype(F32), ((0, 0), (N_HEADS, LANE - 2 * N_HEADS)))
    gdn = (_gdn_body, pc.reshape(bsz, s, GDN_W),
           [gdn_conv_w.astype(F32), pad4(gdn_a_log), pad4(gdn_dt_bias), _row(jnp.tile(gdn_norm, N_HEADS))])
    ya, yb, yc = _mixers_call("mixers", [ssd, rwkv, gdn])

    yd = _sb(pd.reshape(bsz, s, 3 * gw))

    n = bsz * s
    return [ya.reshape(n, gw), yb.reshape(n, gw), yc.reshape(n, gw), yd.reshape(n, gw)]


def kernel(x, p, ffn1_norm, ffn1_w_gate, ffn1_w_up, ffn1_w_down, mix_norm, w_in, ssm_conv_w, ssm_conv_b, ssm_dt_bias, ssm_a_log, ssm_d, ssm_norm, rwkv_mu, rwkv_w0, rwkv_w_up, rwkv_a0, rwkv_a_up, rwkv_g_up, rwkv_k_k, rwkv_k_a, rwkv_r_k, rwkv_ln_w, rwkv_ln_b, gdn_conv_w, gdn_a_log, gdn_dt_bias, gdn_norm, w_out, ffn2_norm, ffn2_w_gate, ffn2_w_up, ffn2_w_down, ple_norm, ple_w_gate, ple_w_proj, final_norm):
    bsz, s, d = x.shape
    depth = p.shape[0]
    n = bsz * s
    tm = min(512, n)
    h = x.reshape(n, d)
    p3 = p.reshape(depth, n, p.shape[-1])
    for i in range(depth):
        h = _ffn(h, _row(ffn1_norm[i]), ffn1_w_gate[i].astype(BF16), ffn1_w_up[i].astype(BF16),
                 ffn1_w_down[i].astype(BF16), tm)
        ys = _mixer(h, bsz, s, mix_norm[i], w_in, i, ssm_conv_w[i], ssm_conv_b[i], ssm_dt_bias[i],
                    ssm_a_log[i], ssm_d[i], ssm_norm[i], rwkv_mu[i], rwkv_w0[i], rwkv_w_up[i],
                    rwkv_a0[i], rwkv_a_up[i], rwkv_g_up[i], rwkv_k_k[i], rwkv_k_a[i], rwkv_r_k[i],
                    rwkv_ln_w[i], rwkv_ln_b[i], gdn_conv_w[i], gdn_a_log[i], gdn_dt_bias[i],
                    gdn_norm[i], tm)
        h = _post(h, ys, p3, i,
                  [w_out[i].astype(BF16), _row(ffn2_norm[i]), ffn2_w_gate[i].astype(BF16),
                   ffn2_w_up[i].astype(BF16), ffn2_w_down[i].astype(BF16), _row(ple_norm[i]),
                   ple_w_gate[i].astype(BF16), ple_w_proj[i].astype(BF16), _row(final_norm)],
                  i == depth - 1, tm)
    return h.reshape(bsz, s, d)
```

```python
import functools

import numpy as np
import jax
import jax.numpy as jnp
from jax import lax
from jax.experimental import pallas as pl
from jax.experimental.pallas import tpu as pltpu

F32 = jnp.float32
BF16 = jnp.bfloat16

HEAD_DIM = 64
N_HEADS = 4
GROUP_W = HEAD_DIM * N_HEADS
CHUNK = 64
N_CHUNKS = 4
CONV_K = 4
SSM_STATE = 128
NORM_EPS = 1e-6
L2_EPS = 1e-6
RWKV_GN_EPS = 64e-5
RWKV_DECAY_OFFSET = 0.5
HALO = 8
LANE = 128
VMEM_LIMIT = 56 * 1024 * 1024


def _dg(a, b, ca, cb):
    return lax.dot_general(a, b, (((ca,), (cb,)), ((), ())), preferred_element_type=F32)


_DIMS = {"nn": (1, 0), "nt": (1, 1), "tn": (0, 0)}


def _mm(a, b, kind="nn"):
    ca, cb = _DIMS[kind]
    return _dg(a.astype(BF16), b.astype(BF16), ca, cb)


def _split2(x):
    hi = x.astype(BF16)
    lo = (x - hi.astype(F32)).astype(BF16)
    return hi, lo


def _split3(x):
    hi = x.astype(BF16)
    r = x - hi.astype(F32)
    mid = r.astype(BF16)
    lo = (r - mid.astype(F32)).astype(BF16)
    return hi, mid, lo


def _mm_sel_l(sel, b, kind="nn"):
    ca, cb = _DIMS[kind]
    h, m, l = _split3(b)
    return _dg(sel, h, ca, cb) + (_dg(sel, m, ca, cb) + _dg(sel, l, ca, cb))


def _mm_sel_r(a, sel, kind="nn"):
    ca, cb = _DIMS[kind]
    h, m, l = _split3(a)
    return _dg(h, sel, ca, cb) + (_dg(m, sel, ca, cb) + _dg(l, sel, ca, cb))


def _mm_sel_r2(a, sel):
    h, l = _split2(a)
    return _dg(h, sel, 1, 0) + _dg(l, sel, 1, 0)


def _sigmoid(x):
    return 1.0 / (1.0 + jnp.exp(-x))


def _silu(x):
    hx = 0.5 * x
    return hx + hx * jnp.tanh(hx)


def _softplus(x):
    return jnp.maximum(x, 0.0) + jnp.log1p(jnp.exp(-jnp.abs(x)))


def _iota(shape, dim):
    return lax.broadcasted_iota(jnp.int32, shape, dim)


def _tile_rows(x, n):
    return jnp.concatenate([x] * n, axis=0)


def _rms(x, gain_row):
    ms = jnp.mean(x * x, axis=-1, keepdims=True)
    return x * lax.rsqrt(ms + NORM_EPS) * gain_row


BDF_SAME, BDF_GRP_STATE = range(2)
BDB_SAME, BDB_GRP_ROWS = range(2)
CAT_NEG, CAT_STRICT, CAT_INCL = range(3)
CATB_EYE, CATB_M8, CATB_OFF8, CATB_OFF16, CATB_OFF32 = range(5)
SEL_ONES_H, SEL_ONES_G, SEL_LTRI = range(3)


def _np_consts():
    r = np.arange(GROUP_W)[:, None]
    c = np.arange(GROUP_W)[None, :]
    same = (r // HEAD_DIM) == (c // HEAD_DIM)

    bdf = np.stack([same, (r // SSM_STATE) == ((c // HEAD_DIM) // 2)]).astype(np.float32)
    bdb = np.stack([same, ((r // HEAD_DIM) // 2) == (c // SSM_STATE)]).astype(np.float32)
    l = np.arange(CHUNK)[:, None]
    s = np.arange(GROUP_W)[None, :] % CHUNK
    cat = np.stack([np.where(l >= s, 0.0, -np.inf), l > s, l >= s]).astype(np.float32)

    def off(b):
        return ((l // (2 * b)) == (s // (2 * b))) & (((l // b) % 2) == 1) & (((s // b) % 2) == 0)

    catb = np.stack([l == s, (l // 8) == (s // 8), off(8), off(16), off(32)]).astype(np.float32)
    t = np.arange(N_CHUNKS * CHUNK)
    ltri = (t[:, None] >= t[None, :]) & ((t[:, None] // CHUNK) == (t[None, :] // CHUNK))
    sel = np.stack([same, (r // SSM_STATE) == (c // SSM_STATE), ltri]).astype(np.float32)
    er = np.arange(LANE)[:, None]
    ec = np.arange(GROUP_W)[None, :] // HEAD_DIM
    expand = np.stack([er == ec, er == ec + N_HEADS]).astype(np.float32)
    pick = np.broadcast_to(np.arange(GROUP_W)[None, :] % HEAD_DIM == 0, (CHUNK, GROUP_W)).astype(np.float32)
    return (jnp.asarray(bdf), jnp.asarray(bdb, BF16), jnp.asarray(cat), jnp.asarray(sel, BF16),
            jnp.asarray(expand, BF16), jnp.asarray(pick, BF16), jnp.asarray(catb, BF16))


def _rows(x_bf, mask_bf):
    return _tile_rows(x_bf, N_HEADS) * mask_bf


def _row_form(colvals, bdb_ref, pick_ref):
    same = bdb_ref[BDB_SAME]
    pick = pick_ref[...]
    parts = [_split3(x) for x in colvals]
    prods = [[_dg(pick, _rows(p, same), 1, 1) for p in ps] for ps in parts]
    return [h + (m + l) for h, m, l in prods]


def _inv_unit_lower(n_cats_bf, catb_ref, same):
    def mul(a_cat_bf, b_cat_bf):
        return _dg(a_cat_bf, _rows(b_cat_bf, same), 1, 0)

    eye = catb_ref[CATB_EYE]
    m8 = catb_ref[CATB_M8]
    d = [n * m8 for n in n_cats_bf]
    d2 = [mul(x, x).astype(BF16) for x in d]
    d4 = [mul(x, x).astype(BF16) for x in d2]
    t = [mul(eye - x, eye + y) for x, y in zip(d, d2)]
    t = [mul(x.astype(BF16), eye + y) for x, y in zip(t, d4)]
    for idx in (CATB_OFF8, CATB_OFF16, CATB_OFF32):
        off = catb_ref[idx]
        tb = [x.astype(BF16) for x in t]
        to = [mul(x, n * off).astype(BF16) for x, n in zip(tb, n_cats_bf)]
        t = [x - mul(y, z) for x, y, z in zip(t, to, tb)]
    return [x.astype(BF16) for x in t]


def _causal_conv(cur, halo, w_ref, first):
    halo = jnp.where(first, 0.0, halo)
    ext = jnp.concatenate([halo, cur], axis=0)
    acc = cur * w_ref[CONV_K - 1:CONV_K, :]
    for j in range(CONV_K - 1):
        sh = pltpu.roll(ext, CONV_K - 1 - j, axis=0)[HALO:]
        acc = acc + sh * w_ref[j:j + 1, :]
    return acc


def _chunk_rows(c):
    return slice(c * CHUNK, (c + 1) * CHUNK)


def _ffn_kernel(h_ref, g_ref, wg_ref, wu_ref, wd_ref, o_ref):
    x = h_ref[...]
    u = _rms(x, g_ref[...]).astype(BF16)
    a = jnp.dot(u, wg_ref[...], preferred_element_type=F32)
    b = jnp.dot(u, wu_ref[...], preferred_element_type=F32)
    act = (_silu(a) * b).astype(BF16)
    y = jnp.dot(act, wd_ref[...], preferred_element_type=F32)
    o_ref[...] = x + 0.5 * y


def _const_spec(shape):
    nd = len(shape)
    return pl.BlockSpec(shape, lambda *_: (0,) * nd, pipeline_mode=pl.Buffered(1))


def _params(sem):
    return pltpu.CompilerParams(dimension_semantics=sem, vmem_limit_bytes=VMEM_LIMIT)


def _ffn(h, gain, wg, wu, wd, tm):
    n, d = h.shape
    f = wg.shape[1]
    return pl.pallas_call(
        _ffn_kernel,
        grid=(n // tm,),
        in_specs=[pl.BlockSpec((tm, d), lambda i: (i, 0)), _const_spec((1, d)),
                  _const_spec((d, f)), _const_spec((d, f)), _const_spec((f, d))],
        out_specs=pl.BlockSpec((tm, d), lambda i: (i, 0)),
        out_shape=jax.ShapeDtypeStruct((n, d), F32),
        compiler_params=_params(("parallel",)),
        name="ffn",
    )(h, gain, wg, wu, wd)


def _inproj_kernel(h_ref, g_ref, w_ref, *refs, bounds):
    o_refs, w_scrs = refs[:len(bounds)], refs[len(bounds):]

    @pl.when(pl.program_id(0) == 0)
    def _():
        for (a, b), w_scr in zip(bounds, w_scrs):
            if w_scr.shape[1] > b - a:
                w_scr[...] = jnp.zeros_like(w_scr)
            w_scr[:, 0:b - a] = w_ref[:, a:b].astype(BF16)

    u = _rms(h_ref[...], g_ref[...]).astype(BF16)
    for w_scr, o_ref in zip(w_scrs, o_refs):
        o_ref[...] = jnp.dot(u, w_scr[...], preferred_element_type=F32).astype(o_ref.dtype)


def _inproj(h, gain, w_all, layer, bounds, widths, tm):
    n, d = h.shape
    n_in = w_all.shape[2]
    dtypes = [F32, F32, F32, BF16]
    return pl.pallas_call(
        functools.partial(_inproj_kernel, bounds=bounds),
        grid=(n // tm,),
        in_specs=[pl.BlockSpec((tm, d), lambda i: (i, 0)), _const_spec((1, d)),
                  pl.BlockSpec((None, d, n_in), lambda i: (layer, 0, 0), pipeline_mode=pl.Buffered(1))],
        out_specs=[pl.BlockSpec((tm, wd), lambda i: (i, 0)) for wd in widths],
        out_shape=[jax.ShapeDtypeStruct((n, wd), dt) for wd, dt in zip(widths, dtypes)],
        scratch_shapes=[pltpu.VMEM((d, wd), BF16) for wd in widths],
        compiler_params=_params(("arbitrary",)),
        name="inproj",
    )(h, gain, w_all)


def _post_kernel(h_ref, ya_ref, yb_ref, yc_ref, yd_ref, p_ref, wo_ref, g2_ref, wg_ref, wu_ref, wd_ref,
                 gp_ref, wpg_ref, wpp_ref, fg_ref, o_ref, *, final):
    x = h_ref[...]
    for i, y_ref in enumerate((ya_ref, yb_ref, yc_ref, yd_ref)):
        x = x + jnp.dot(y_ref[...], wo_ref[i * GROUP_W:(i + 1) * GROUP_W, :], preferred_element_type=F32)
    u = _rms(x, g2_ref[...]).astype(BF16)
    a = jnp.dot(u, wg_ref[...], preferred_element_type=F32)
    b = jnp.dot(u, wu_ref[...], preferred_element_type=F32)
    act = (_silu(a) * b).astype(BF16)
    x = x + 0.5 * jnp.dot(act, wd_ref[...], preferred_element_type=F32)
    u = _rms(x, gp_ref[...]).astype(BF16)
    gate = _sigmoid(jnp.dot(u, wpg_ref[...], preferred_element_type=F32))
    e = jnp.dot(p_ref[...].astype(BF16), wpp_ref[...], preferred_element_type=F32)
    y = x + e * gate
    if final:
        y = _rms(y, fg_ref[...])
    o_ref[...] = y


def _post(h, ys, p3, layer, consts, final, tm):
    n, d = h.shape
    pd = p3.shape[2]
    return pl.pallas_call(
        functools.partial(_post_kernel, final=final),
        grid=(n // tm,),
        in_specs=[pl.BlockSpec((tm, d), lambda i: (i, 0))]
                 + [pl.BlockSpec((tm, GROUP_W), lambda i: (i, 0)) for _ in ys]
                 + [pl.BlockSpec((None, tm, pd), lambda i: (layer, i, 0))]
                 + [_const_spec(c.shape) for c in consts],
        out_specs=pl.BlockSpec((tm, d), lambda i: (i, 0)),
        out_shape=jax.ShapeDtypeStruct((n, d), F32),
        compiler_params=_params(("parallel",)),
        name="post",
    )(h, *ys, p3, *consts)


def _halo_map(rows_per_block):
    step = rows_per_block // HALO
    return lambda b, i: (b, jnp.maximum(i * step - 1, 0), 0)


N_CONSTS = 7


def _mixers_kernel(*refs, bodies, n_params):
    k = len(bodies)
    consts = refs[2 * k:2 * k + N_CONSTS]
    pos = 2 * k + N_CONSTS
    params = []
    for n in n_params:
        params.append(refs[pos:pos + n])
        pos += n
    outs = refs[pos:pos + k]
    states = refs[pos + k:pos + 2 * k]

    @pl.when(pl.program_id(1) == 0)
    def _():
        for st_ref in states:
            st_ref[...] = jnp.zeros_like(st_ref)

    gens = [body(refs[2 * i], refs[2 * i + 1], *consts, *params[i], outs[i], states[i])
            for i, body in enumerate(bodies)]
    wanted = [next(g) for g in gens]
    flat = [n for w in wanted for n in w]
    bdb_ref, catb_ref = consts[1], consts[6]
    t_cats = _inv_unit_lower(flat, catb_ref, bdb_ref[BDB_SAME]) if flat else []
    pos = 0
    for g, w in zip(gens, wanted):
        try:
            g.send(t_cats[pos:pos + len(w)])
        except StopIteration:
            pos += len(w)
        else:
            raise AssertionError("a mixer body yields exactly once")


def _mixers_call(name, mixers):
    bsz, s, _ = mixers[0][1].shape
    t = N_CHUNKS * CHUNK
    consts = _np_consts()
    assert len(consts) == N_CONSTS
    in_specs, args = [], []
    for _, x, _ in mixers:
        w = x.shape[2]
        in_specs += [pl.BlockSpec((1, t, w), lambda b, i: (b, i, 0)), pl.BlockSpec((1, HALO, w), _halo_map(t))]
        args += [x, x]
    in_specs += [_const_spec(c.shape) for c in consts]
    args += list(consts)
    for _, _, params in mixers:
        in_specs += [_const_spec(p.shape) for p in params]
        args += list(params)
    k = len(mixers)
    return pl.pallas_call(
        functools.partial(_mixers_kernel, bodies=[m[0] for m in mixers],
                          n_params=[len(m[2]) for m in mixers]),
        grid=(bsz, s // t),
        in_specs=in_specs,
        out_specs=[pl.BlockSpec((1, t, GROUP_W), lambda b, i: (b, i, 0))] * k,
        out_shape=[jax.ShapeDtypeStruct((bsz, s, GROUP_W), BF16)] * k,
        scratch_shapes=[pltpu.VMEM((GROUP_W, GROUP_W), F32)] * k,
        compiler_params=_params(("parallel", "arbitrary")),
        name=name,
    )(*args)


SSD_W = 1152


def _ssd_body(x_ref, halo_ref, bdf_ref, bdb_ref, cat_ref, sel_ref, exp_ref, pick_ref, catb_ref,
              cw_ref, cb_ref, dtb_ref, alog_ref, dsk_ref, nw_ref, o_ref, st_ref):
    yield []
    first = pl.program_id(1) == 0
    x = x_ref[0]
    z = x[:, 0:GROUP_W]
    xbc = _causal_conv(x[:, GROUP_W:4 * GROUP_W], halo_ref[0][:, GROUP_W:4 * GROUP_W], cw_ref, first)
    xbc = _silu(xbc + cb_ref[...])
    xs = xbc[:, 0:GROUP_W]
    bm = xbc[:, GROUP_W:2 * GROUP_W].astype(BF16)
    cm = xbc[:, 2 * GROUP_W:3 * GROUP_W].astype(BF16)
    dt_pad = _softplus(x[:, 4 * GROUP_W:] + dtb_ref[...])
    la_pad = dt_pad * (-jnp.exp(alog_ref[...]))
    dt_b = _mm_sel_r(dt_pad, exp_ref[0])
    la_b = _mm_sel_r(la_pad, exp_ref[0])
    cs = _mm_sel_l(sel_ref[SEL_LTRI], la_b)
    xc = xs * dt_b
    ecs = jnp.exp(cs)
    same = bdb_ref[BDB_SAME]
    grp_rows = bdb_ref[BDB_GRP_ROWS]
    grp_state = bdf_ref[BDF_GRP_STATE]
    chunks = [_chunk_rows(c) for c in range(N_CHUNKS)]
    neg = cat_ref[CAT_NEG]
    rforms = _row_form([cs[rs] for rs in chunks], bdb_ref, pick_ref)
    segs = [jnp.exp(cs[rs] - rf + neg) for rs, rf in zip(chunks, rforms)]
    scores = [(_dg(cm[rs], _rows(bm[rs], grp_rows), 1, 1) * sg).astype(BF16)
              for rs, sg in zip(chunks, segs)]
    xc_bf = xc.astype(BF16)
    ys = [_dg(sc, _rows(xc_bf[rs], same), 1, 0) for rs, sc in zip(chunks, scores)]
    lasts = [cs[rs][CHUNK - 1:CHUNK, :] for rs in chunks]
    upds = [_dg(bm[rs], (xc[rs] * jnp.exp(last - cs[rs])).astype(BF16), 0, 0) * grp_state
            for rs, last in zip(chunks, lasts)]
    st = st_ref[...]
    for c, rs in enumerate(chunks):
        ys[c] = ys[c] + _dg(cm[rs], st.astype(BF16), 1, 0) * ecs[rs]
        st = st * jnp.exp(lasts[c]) + upds[c]
    st_ref[...] = st
    y = (jnp.concatenate(ys, axis=0) + dsk_ref[...] * xs) * _silu(z)
    ms = _mm(y * y, sel_ref[SEL_ONES_G]) * (1.0 / SSM_STATE)
    o_ref[0] = (y * lax.rsqrt(ms + NORM_EPS) * nw_ref[...]).astype(o_ref.dtype)


GDN_W = 1152


def _gdn_body(x_ref, halo_ref, bdf_ref, bdb_ref, cat_ref, sel_ref, exp_ref, pick_ref, catb_ref,
              cw_ref, alog_ref, dtb_ref, nw_ref, o_ref, st_ref):
    first = pl.program_id(1) == 0
    ones_h = sel_ref[SEL_ONES_H]
    x = x_ref[0]
    qkv = _silu(_causal_conv(x[:, 0:3 * GROUP_W], halo_ref[0][:, 0:3 * GROUP_W], cw_ref, first))
    z = x[:, 3 * GROUP_W:4 * GROUP_W]
    ba = x[:, 4 * GROUP_W:]
    q = qkv[:, 0:GROUP_W]
    k = qkv[:, GROUP_W:2 * GROUP_W]
    v = qkv[:, 2 * GROUP_W:3 * GROUP_W]
    q = q * lax.rsqrt(_mm(q * q, ones_h) + L2_EPS) * (HEAD_DIM ** -0.5)
    k = k * lax.rsqrt(_mm(k * k, ones_h) + L2_EPS)
    beta_pad = _sigmoid(ba)
    g_pad = -jnp.exp(alog_ref[...]) * _softplus(ba + dtb_ref[...])
    beta_b = _mm_sel_r2(beta_pad, exp_ref[0])
    g_b = _mm_sel_r(g_pad, exp_ref[1])
    gc = _mm_sel_l(sel_ref[SEL_LTRI], g_b)
    eg = jnp.exp(gc)
    kb = k * beta_b
    q_bf = q.astype(BF16)
    k_bf = k.astype(BF16)
    kb_bf = kb.astype(BF16)
    vb_bf = (v * beta_b).astype(BF16)
    kbg_bf = (kb * eg).astype(BF16)
    qg_bf = (q * eg).astype(BF16)
    same = bdb_ref[BDB_SAME]
    chunks = [_chunk_rows(c) for c in range(N_CHUNKS)]
    neg = cat_ref[CAT_NEG]
    strict = cat_ref[CAT_STRICT]
    rforms = _row_form([gc[rs] for rs in chunks], bdb_ref, pick_ref)
    decays = [jnp.exp(gc[rs] - rf + neg) for rs, rf in zip(chunks, rforms)]
    k_rows = [_rows(k_bf[rs], same) for rs in chunks]
    a_cats = [(_dg(kb_bf[rs], kr, 1, 1) * dc * strict).astype(BF16)
              for rs, kr, dc in zip(chunks, k_rows, decays)]
    qks = [(_dg(q_bf[rs], kr, 1, 1) * dc).astype(BF16) for rs, kr, dc in zip(chunks, k_rows, decays)]
    t_cats = yield a_cats
    uws = [_dg(t_cat, jnp.concatenate([_rows(vb_bf[rs], same), _rows(kbg_bf[rs], same)], axis=1), 1, 0)
           for rs, t_cat in zip(chunks, t_cats)]
    lasts = [gc[rs][CHUNK - 1:CHUNK, :] for rs in chunks]
    k_decs = [(k[rs] * jnp.exp(last - gc[rs])).astype(BF16) for rs, last in zip(chunks, lasts)]
    same_f = bdf_ref[BDF_SAME]
    us = [uw[:, 0:GROUP_W] for uw in uws]
    ws_bf = [uw[:, GROUP_W:].astype(BF16) for uw in uws]
    m_st = [(-same_f * _dg(kd, w, 0, 0)).astype(BF16) for kd, w in zip(k_decs, ws_bf)]
    c_st = [same_f * _dg(kd, u.astype(BF16), 0, 0) for kd, u in zip(k_decs, us)]
    st = st_ref[...]
    sts = []
    for c in range(N_CHUNKS):
        st_bf = st.astype(BF16)
        sts.append(st_bf)
        st = st * jnp.exp(lasts[c]) + _dg(m_st[c], st_bf, 1, 0) + c_st[c]
    st_ref[...] = st
    v_news = [(u - _dg(w, s_bf, 1, 0)).astype(BF16) for u, w, s_bf in zip(us, ws_bf, sts)]
    os_ = [_dg(qg_bf[rs], s_bf, 1, 0) + _dg(qk, _rows(vn, same), 1, 0)
           for rs, s_bf, qk, vn in zip(chunks, sts, qks, v_news)]
    o = jnp.concatenate(os_, axis=0)
    ms = _mm(o * o, ones_h) * (1.0 / HEAD_DIM)
    o_ref[0] = (o * lax.rsqrt(ms + NORM_EPS) * nw_ref[...] * _silu(z)).astype(o_ref.dtype)


RWKV_W = 896


def _rwkv_body(x_ref, halo_ref, bdf_ref, bdb_ref, cat_ref, sel_ref, exp_ref, pick_ref, catb_ref,
               mu_ref, w0_ref, wup_ref, a0_ref, aup_ref, gup_ref, kk_ref, ka_ref,
               rk_ref, lnw_ref, lnb_ref, o_ref, st_ref):
    first = pl.program_id(1) == 0
    ones_h = sel_ref[SEL_ONES_H]
    x = x_ref[0]
    halo = jnp.where(first, 0.0, halo_ref[0])
    shifted = pltpu.roll(jnp.concatenate([halo, x], axis=0), 1, axis=0)[HALO:]
    x = x + mu_ref[...] * (shifted - x)
    r = x[:, 0:GROUP_W]
    k = x[:, GROUP_W:2 * GROUP_W]
    v = x[:, 2 * GROUP_W:3 * GROUP_W]
    lora = x[:, 3 * GROUP_W:]
    w_log = -_softplus(-(w0_ref[...] + _mm(jnp.tanh(lora), wup_ref[...]))) - RWKV_DECAY_OFFSET
    lw = -jnp.exp(w_log)
    a_gate = _sigmoid(a0_ref[...] + _mm(lora, aup_ref[...]))
    g = _mm(_sigmoid(lora), gup_ref[...])
    kk = k * kk_ref[...]
    kk = kk * lax.rsqrt(_mm(kk * kk, ones_h) + L2_EPS)
    k = k * (1.0 + (a_gate - 1.0) * ka_ref[...])
    b_v = kk * a_gate
    cl = _mm_sel_l(sel_ref[SEL_LTRI], lw)
    e_in = jnp.exp(cl)
    e_inv = jnp.exp(-cl)
    at_bf = (-kk * jnp.exp(cl - lw)).astype(BF16)
    rt_bf = (r * e_in).astype(BF16)
    bt_bf = (b_v * e_inv).astype(BF16)
    kt_bf = (k * e_inv).astype(BF16)
    v_bf = v.astype(BF16)
    same = bdb_ref[BDB_SAME]
    strict = cat_ref[CAT_STRICT]
    incl = cat_ref[CAT_INCL]
    chunks = [_chunk_rows(c) for c in range(N_CHUNKS)]
    zero = jnp.zeros((), F32)
    b_rows = [_rows(bt_bf[rs], same) for rs in chunks]
    k_rows = [_rows(kt_bf[rs], same) for rs in chunks]
    v_rows = [_rows(v_bf[rs], same) for rs in chunks]
    n_cats = [jnp.where(strict > 0, -_dg(at_bf[rs], br, 1, 1), zero).astype(BF16)
              for rs, br in zip(chunks, b_rows)]
    aks = [jnp.where(strict > 0, _dg(at_bf[rs], kr, 1, 1), zero).astype(BF16)
           for rs, kr in zip(chunks, k_rows)]
    rbs = [jnp.where(incl > 0, _dg(rt_bf[rs], br, 1, 1), zero).astype(BF16)
           for rs, br in zip(chunks, b_rows)]
    rks = [jnp.where(incl > 0, _dg(rt_bf[rs], kr, 1, 1), zero).astype(BF16)
           for rs, kr in zip(chunks, k_rows)]
    t_cats = yield n_cats
    tas = [_dg(t_cat, _rows(at_bf[rs], same), 1, 0).astype(BF16) for rs, t_cat in zip(chunks, t_cats)]
    tvs = [_dg(t_cat, _rows(_dg(ak, vr, 1, 0).astype(BF16), same), 1, 0)
           for t_cat, ak, vr in zip(t_cats, aks, v_rows)]
    y_vs = [_dg(rk, vr, 1, 0) for rk, vr in zip(rks, v_rows)]
    lasts = [cl[rs][CHUNK - 1:CHUNK, :] for rs in chunks]
    to_ends = [jnp.exp(last - cl[rs]) for rs, last in zip(chunks, lasts)]
    same_f = bdf_ref[BDF_SAME]
    b_ends = [(b_v[rs] * te).astype(BF16) for rs, te in zip(chunks, to_ends)]
    k_ends = [(k[rs] * te).astype(BF16) for rs, te in zip(chunks, to_ends)]
    m_st = [(same_f * _dg(ta, be, 0, 0)).astype(BF16) for ta, be in zip(tas, b_ends)]
    c_st = [same_f * _dg(jnp.concatenate([tv.astype(BF16), v_bf[rs]], axis=0),
                         jnp.concatenate([be, ke], axis=0), 0, 0)
            for tv, rs, be, ke in zip(tvs, chunks, b_ends, k_ends)]
    st = st_ref[...]
    sts = []
    for c in range(N_CHUNKS):
        st_bf = st.astype(BF16)
        sts.append(st_bf)
        st = st * jnp.exp(lasts[c]) + _dg(st_bf, m_st[c], 1, 0) + c_st[c]
    st_ref[...] = st
    us = [(_dg(ta, s_bf, 1, 1) + tv).astype(BF16) for ta, tv, s_bf in zip(tas, tvs, sts)]
    ys = [_dg(rt_bf[rs], s_bf, 1, 1) + _dg(rb, _rows(u, same), 1, 0) + yv
          for rs, s_bf, rb, u, yv in zip(chunks, sts, rbs, us, y_vs)]
    y = jnp.concatenate(ys, axis=0)
    inv_d = 1.0 / HEAD_DIM
    mean = _mm(y, ones_h) * inv_d
    yc = y - mean
    var = _mm(yc * yc, ones_h) * inv_d
    yn = yc * lax.rsqrt(var + RWKV_GN_EPS) * lnw_ref[...] + lnb_ref[...]
    bonus = _mm(r * k * rk_ref[...], ones_h) * v
    o_ref[0] = ((yn + bonus) * g).astype(o_ref.dtype)


SB_BLOCK = 256
SB_NEGLIGIBLE = 128.0


def _neg_abs(x):
    sign = jnp.uint32(0x80000000)
    return lax.bitcast_convert_type(lax.bitcast_convert_type(x, jnp.uint32) | sign, F32)


def _sb_kernel(q_ref, k_ref, v_ref, o_ref, acc_ref, run_ref, zz_ref, w_ref):
    qi = pl.program_id(1)
    t = SB_BLOCK
    heads = range(N_HEADS)
    scale = HEAD_DIM ** -0.5
    assert scale == 0.125
    lane_head = _iota((t, GROUP_W), 1) // HEAD_DIM
    q = q_ref[0] * jnp.asarray(scale, q_ref.dtype)
    zero = jnp.zeros((), q.dtype)
    q_h = [jnp.where(lane_head == h, q, zero) for h in heads]
    tri = (_iota((t, t), 0) >= _iota((t, t), 1)).astype(BF16)
    upper2 = jnp.concatenate([tri, tri], axis=0)
    causal = _iota((t, t), 1) < _iota((t, t), 0)
    acc_ref[...] = jnp.zeros_like(acc_ref)
    run_ref[...] = jnp.zeros_like(run_ref)

    def scores(j):
        kj = k_ref[0, pl.ds(pl.multiple_of(j * t, t), t), :]
        return [_dg(q_h[h], kj, 1, 1) for h in heads]

    def weights(zz, masked):
        parts, sums = [], []
        for h in heads:
            sp = jnp.maximum(zz[h], 0.0) + jnp.log(1.0 + jnp.exp(_neg_abs(zz[h])))
            if masked:
                sp = jnp.where(causal, sp, 0.0)
            hi, lo = _split2(sp)
            parts.append(jnp.concatenate([hi, lo], axis=1))
            sums.append(jnp.sum(sp, axis=1, keepdims=True))
        later = [_dg(p, upper2, 1, 0) for p in parts]
        wgts = []
        for h in heads:
            run = run_ref[h]
            wgt = jnp.exp(zz[h] - later[h] - jnp.concatenate([run] * (t // LANE), axis=1))
            if masked:
                wgt = jnp.where(causal, wgt, 0.0)
            wgts.append(wgt.astype(BF16))
            run_ref[h] = run + sums[h]
        return jnp.concatenate(wgts, axis=1)

    def accumulate(w_cat, j):
        vj = v_ref[0, pl.ds(pl.multiple_of(j * t, t), t), :]
        v_cat = jnp.concatenate([jnp.where(lane_head == h, vj, zero) for h in heads], axis=0)
        acc_ref[...] += _dg(w_cat, v_cat, 1, 0)

    def stash(zz, w_cat):
        for h in heads:
            zz_ref[h] = zz[h]
        w_ref[...] = w_cat

    zz_next = scores(jnp.maximum(qi - 1, 0))
    stash(zz_next, weights(scores(qi), True))

    def run_min():
        r = run_ref[0]
        for h in range(1, N_HEADS):
            r = jnp.minimum(r, run_ref[h])
        return jnp.min(r)

    def cond(carry):
        i, rmin = carry
        return i <= qi

    def body(carry):
        i, _ = carry
        j = qi - i
        accumulate(w_ref[...], j + 1)
        zz_next = scores(jnp.maximum(j - 1, 0))
        stash(zz_next, weights([zz_ref[h] for h in heads], False))
        return i + 1, run_min()

    i_end, _ = lax.while_loop(cond, body, (jnp.int32(1), run_min()))
    accumulate(w_ref[...], qi - (i_end - 1))
    o_ref[0] = acc_ref[...].astype(o_ref.dtype)


def _sb(pd):
    bsz, s, _ = pd.shape
    t = SB_BLOCK
    return pl.pallas_call(
        _sb_kernel,
        grid=(bsz, s // t),
        in_specs=[pl.BlockSpec((1, t, GROUP_W), lambda b, i: (b, i, 0)),
                  pl.BlockSpec((1, s, GROUP_W), lambda b, i: (b, 0, 1)),
                  pl.BlockSpec((1, s, GROUP_W), lambda b, i: (b, 0, 2))],
        out_specs=pl.BlockSpec((1, t, GROUP_W), lambda b, i: (b, i, 0)),
        out_shape=jax.ShapeDtypeStruct((bsz, s, GROUP_W), BF16),
        scratch_shapes=[pltpu.VMEM((t, GROUP_W), F32), pltpu.VMEM((N_HEADS, t, LANE), F32),
                        pltpu.VMEM((N_HEADS, t, t), F32), pltpu.VMEM((t, N_HEADS * t), BF16)],
        compiler_params=_params(("parallel", "arbitrary")),
        name="stickbreak",
    )(pd, pd, pd)


def _row(v, width=None):
    v = v.reshape(1, -1).astype(F32)
    if width is not None and v.shape[1] < width:
        v = jnp.pad(v, ((0, 0), (0, width - v.shape[1])))
    return v


def _pad_rows_at(w, offset, total):
    return jnp.pad(w, ((offset, total - offset - w.shape[0]), (0, 0)))


def _mixer(h2, bsz, s, gain, w_in, layer, ssm_conv_w, ssm_conv_b, ssm_dt_bias, ssm_a_log, ssm_d, ssm_norm,
           rwkv_mu, rwkv_w0, rwkv_w_up, rwkv_a0, rwkv_a_up, rwkv_g_up, rwkv_k_k, rwkv_k_a, rwkv_r_k,
           rwkv_ln_w, rwkv_ln_b, gdn_conv_w, gdn_a_log, gdn_dt_bias, gdn_norm, tm):
    gw = GROUP_W
    ssm_in = 4 * gw + N_HEADS
    c0 = ssm_in
    c1 = c0 + RWKV_W
    c2 = c1 + 4 * gw + 2 * N_HEADS
    bounds = ((0, c0), (c0, c1), (c1, c2), (c2, w_in.shape[2]))
    pa, pb, pc, pd = _inproj(h2, _row(gain), w_in, layer, bounds, (SSD_W, RWKV_W, GDN_W, 3 * gw), tm)

    ssd = (_ssd_body, pa.reshape(bsz, s, SSD_W),
           [ssm_conv_w.astype(F32), _row(ssm_conv_b), _row(ssm_dt_bias, LANE),
            _row(ssm_a_log, LANE), _row(jnp.repeat(ssm_d, HEAD_DIM)), _row(ssm_norm)])
    lora_w = RWKV_W - 3 * gw
    rwkv = (_rwkv_body, pb.reshape(bsz, s, RWKV_W),
            [_row(rwkv_mu), _row(rwkv_w0),
             _pad_rows_at(rwkv_w_up, 0, lora_w).astype(BF16), _row(rwkv_a0),
             _pad_rows_at(rwkv_a_up, rwkv_w_up.shape[0], lora_w).astype(BF16),
             _pad_rows_at(rwkv_g_up, rwkv_w_up.shape[0] + rwkv_a_up.shape[0], lora_w).astype(BF16),
             _row(rwkv_k_k), _row(rwkv_k_a), _row(rwkv_r_k), _row(rwkv_ln_w), _row(rwkv_ln_b)])
    pad4 = lambda t: jnp.pad(t.reshape(1, -1).astype(F32), ((0, 0), (N_HEADS, LANE - 2 * N_HEADS)))
    gdn = (_gdn_body, pc.reshape(bsz, s, GDN_W),
           [gdn_conv_w.astype(F32), pad4(gdn_a_log), pad4(gdn_dt_bias), _row(jnp.tile(gdn_norm, N_HEADS))])
    ya, yb, yc = _mixers_call("mixers", [ssd, rwkv, gdn])

    yd = _sb(pd.reshape(bsz, s, 3 * gw))

    n = bsz * s
    return [ya.reshape(n, gw), yb.reshape(n, gw), yc.reshape(n, gw), yd.reshape(n, gw)]


def kernel(x, p, ffn1_norm, ffn1_w_gate, ffn1_w_up, ffn1_w_down, mix_norm, w_in, ssm_conv_w, ssm_conv_b, ssm_dt_bias, ssm_a_log, ssm_d, ssm_norm, rwkv_mu, rwkv_w0, rwkv_w_up, rwkv_a0, rwkv_a_up, rwkv_g_up, rwkv_k_k, rwkv_k_a, rwkv_r_k, rwkv_ln_w, rwkv_ln_b, gdn_conv_w, gdn_a_log, gdn_dt_bias, gdn_norm, w_out, ffn2_norm, ffn2_w_gate, ffn2_w_up, ffn2_w_down, ple_norm, ple_w_gate, ple_w_proj, final_norm):
    bsz, s, d = x.shape
    depth = p.shape[0]
    n = bsz * s
    tm = min(512, n)
    h = x.reshape(n, d)
    p3 = p.reshape(depth, n, p.shape[-1])
    for i in range(depth):
        h = _ffn(h, _row(ffn1_norm[i]), ffn1_w_gate[i].astype(BF16), ffn1_w_up[i].astype(BF16),
                 ffn1_w_down[i].astype(BF16), tm)
        ys = _mixer(h, bsz, s, mix_norm[i], w_in, i, ssm_conv_w[i], ssm_conv_b[i], ssm_dt_bias[i],
                    ssm_a_log[i], ssm_d[i], ssm_norm[i], rwkv_mu[i], rwkv_w0[i], rwkv_w_up[i],
                    rwkv_a0[i], rwkv_a_up[i], rwkv_g_up[i], rwkv_k_k[i], rwkv_k_a[i], rwkv_r_k[i],
                    rwkv_ln_w[i], rwkv_ln_b[i], gdn_conv_w[i], gdn_a_log[i], gdn_dt_bias[i],
                    gdn_norm[i], tm)
        h = _post(h, ys, p3, i,
                  [w_out[i].astype(BF16), _row(ffn2_norm[i]), ffn2_w_gate[i].astype(BF16),
                   ffn2_w_up[i].astype(BF16), ffn2_w_down[i].astype(BF16), _row(ple_norm[i]),
                   ple_w_gate[i].astype(BF16), ple_w_proj[i].astype(BF16), _row(final_norm)],
                  i == depth - 1, tm)
    return h.reshape(bsz, s, d)
```
